```python
import math
import jax, jax.numpy as jnp
from jax import lax
import numpy as np

D_MODEL = 1024
BATCH = 2
SEQ = 8192
DEPTH = 1

SSD_EXPAND = 2
D_INNER = SSD_EXPAND * D_MODEL
SSD_HEAD_DIM = 64
SSD_HEADS = D_INNER // SSD_HEAD_DIM
SSD_GROUPS = 4
D_STATE = 128
CONV_WIDTH = 4
SSD_CHUNK = 128
D_XBC = D_INNER + 2 * SSD_GROUPS * D_STATE
DIFF_HEADS = 8
DIFF_HEAD_DIM = 64
DIFF_V_DIM = 2 * DIFF_HEAD_DIM
DIFF_QK = DIFF_HEADS * 2 * DIFF_HEAD_DIM
DIFF_V = DIFF_HEADS * DIFF_V_DIM
Q_BLOCK = 128
MEM_LEN = 256
MEM_HEADS = 4
MEM_HEAD_DIM = D_MODEL // MEM_HEADS
PEER_HEADS = 8
PEER_N_KEYS = 128
PEER_N_EXPERTS = PEER_N_KEYS * PEER_N_KEYS
PEER_D_KEY = 256
PEER_HALF = PEER_D_KEY // 2
PEER_TOPK = 16
PEER_BLOCK = 128
ALPHA = (2 * DEPTH) ** 0.25
BETA = (8 * DEPTH) ** -0.25
LN_EPS = 1e-5
RMS_EPS = 1e-5
D_IN_PROJ = D_INNER + D_XBC + SSD_HEADS + 2 * DIFF_QK + DIFF_V + 2 * D_MODEL

kernel_name = "hybrid_ssd_diffattn_peer_block"


def _split_cols(h):
    sizes = (D_INNER, D_XBC, SSD_HEADS, DIFF_QK, DIFF_QK, DIFF_V, D_MODEL, D_MODEL)
    offs = np.cumsum(sizes)[:-1].tolist()
    return jnp.split(h, offs, axis=-1)


def layer_norm(x, g, b):
    xf = x.astype(jnp.float32)
    mu = jnp.mean(xf, axis=-1, keepdims=True)
    var = jnp.mean(jnp.square(xf - mu), axis=-1, keepdims=True)
    return ((xf - mu) * lax.rsqrt(var + LN_EPS) * g + b).astype(x.dtype)


def rms_norm(x, w):
    xf = x.astype(jnp.float32)
    xf = xf * lax.rsqrt(jnp.mean(jnp.square(xf), axis=-1, keepdims=True) + RMS_EPS)
    return (xf * w).astype(x.dtype)


def causal_dwconv(x, w, b):
    y = lax.conv_general_dilated(
        x, w[:, None, :], window_strides=(1,), padding=[(CONV_WIDTH - 1, 0)],
        dimension_numbers=("NWC", "WIO", "NWC"), feature_group_count=x.shape[-1])
    return y + b


def ssd_chunked(X, dtA, Bm, Cm):
    b, l, h, p = X.shape
    g, n = Bm.shape[2], Bm.shape[3]
    e = h // g
    c = l // SSD_CHUNK
    L = SSD_CHUNK
    X = X.reshape(b, c, L, g, e, p)
    A = dtA.astype(jnp.float32).reshape(b, c, L, g, e)
    Bm = Bm.reshape(b, c, L, g, n)
    Cm = Cm.reshape(b, c, L, g, n)
    Acs = jnp.cumsum(A, axis=2)
    AcsT = jnp.moveaxis(Acs, 2, -1)
    seg = AcsT[..., :, None] - AcsT[..., None, :]
    tril = jnp.tril(jnp.ones((L, L), dtype=bool))
    decay = jnp.exp(jnp.where(tril, seg, -jnp.inf))
    CB = jnp.einsum("bclgn,bcsgn->bcgls", Cm, Bm)
    M = CB[:, :, :, None] * decay
    y_diag = jnp.einsum("bcgels,bcsgep->bclgep", M, X)
    decay_states = jnp.exp(Acs[:, :, -1:] - Acs)
    states = jnp.einsum("bclgn,bclgep->bcgepn", Bm, X * decay_states[..., None])
    chunk_decay = jnp.exp(Acs[:, :, -1])

    def step(h_prev, inp):
        st, dec = inp
        return h_prev * dec[..., None, None] + st, h_prev

    h0 = jnp.zeros_like(states[:, 0])
    _, prev = lax.scan(step, h0, (jnp.moveaxis(states, 1, 0), jnp.moveaxis(chunk_decay, 1, 0)))
    prev = jnp.moveaxis(prev, 0, 1)
    y_off = jnp.einsum("bclgn,bcgepn->bclgep", Cm, prev) * jnp.exp(Acs)[..., None]
    return (y_diag + y_off).reshape(b, l, h, p)


def mamba2_branch(z, xbc, dt_raw, conv_w, conv_b, dt_bias, a_log, d_skip, norm_w):
    bsz, seq, _ = z.shape
    xbc = jax.nn.silu(causal_dwconv(xbc, conv_w, conv_b))
    xs, bm, cm = jnp.split(xbc, [D_INNER, D_INNER + SSD_GROUPS * D_STATE], axis=-1)
    dt = jax.nn.softplus(dt_raw.astype(jnp.float32) + dt_bias.astype(jnp.float32))
    a = -jnp.exp(a_log.astype(jnp.float32))
    xh = xs.reshape(bsz, seq, SSD_HEADS, SSD_HEAD_DIM)
    y = ssd_chunked(xh * dt[..., None], dt * a,
                    bm.reshape(bsz, seq, SSD_GROUPS, D_STATE),
                    cm.reshape(bsz, seq, SSD_GROUPS, D_STATE))
    y = y + d_skip[:, None] * xh
    y = y.reshape(bsz, seq, D_INNER) * jax.nn.silu(z)
    yg = y.astype(jnp.float32).reshape(bsz, seq, SSD_GROUPS, D_INNER // SSD_GROUPS)
    yg = yg * lax.rsqrt(jnp.mean(jnp.square(yg), axis=-1, keepdims=True) + RMS_EPS)
    return (yg.reshape(bsz, seq, D_INNER) * norm_w).astype(z.dtype)


def diff_attention(q, k, v, lam, lambda_init, subln_w):
    bsz, seq, _ = q.shape
    nb = seq // Q_BLOCK
    qb = q.reshape(bsz, nb, Q_BLOCK, DIFF_HEADS, 2, DIFF_HEAD_DIM).transpose(1, 0, 3, 4, 2, 5)
    qb = qb * (DIFF_HEAD_DIM ** -0.5)
    kt = k.reshape(bsz, seq, DIFF_HEADS, 2, DIFF_HEAD_DIM).transpose(0, 2, 3, 1, 4)
    vt = v.reshape(bsz, seq, DIFF_HEADS, DIFF_V_DIM).transpose(0, 2, 1, 3)
    key_pos = jnp.arange(seq)

    def one_block(args):
        qblk, blk = args
        s = jnp.einsum("bhmqd,bhmkd->bhmqk", qblk, kt).astype(jnp.float32)
        q_pos = blk * Q_BLOCK + jnp.arange(Q_BLOCK)
        s = jnp.where(key_pos[None, :] <= q_pos[:, None], s, -jnp.inf)
        p = jax.nn.softmax(s, axis=-1)
        att = p[:, :, 0] - lam * p[:, :, 1]
        return jnp.einsum("bhqk,bhkv->bhqv", att.astype(vt.dtype), vt)

    o = lax.map(one_block, (qb, jnp.arange(nb)))
    o = o.transpose(1, 0, 3, 2, 4).reshape(bsz, seq, DIFF_HEADS, DIFF_V_DIM)
    o = rms_norm(o, subln_w) * (1.0 - lambda_init)
    return o.reshape(bsz, seq, DIFF_V)


def memory_cross_attention(x, mem, wq, wk, wv, wo):
    bsz, seq, _ = x.shape
    m = mem.shape[1]
    q = (x @ wq).reshape(bsz, seq, MEM_HEADS, MEM_HEAD_DIM)
    k = (mem @ wk).reshape(bsz, m, MEM_HEADS, MEM_HEAD_DIM)
    v = (mem @ wv).reshape(bsz, m, MEM_HEADS, MEM_HEAD_DIM)
    s = jnp.einsum("bshd,bmhd->bhsm", q, k).astype(jnp.float32) * (MEM_HEAD_DIM ** -0.5)
    p = jax.nn.softmax(s, axis=-1).astype(x.dtype)
    o = jnp.einsum("bhsm,bmhd->bshd", p, v).reshape(bsz, seq, D_MODEL)
    return o @ wo


def peer_ffn(x, wq, sub_keys, peer_u, peer_v):
    bsz, seq, _ = x.shape
    q = (x @ wq).reshape(bsz, seq, PEER_HEADS, 2, PEER_HALF)
    s = jnp.einsum("bshtd,htkd->bshtk", q, sub_keys).astype(jnp.float32)
    top_s, top_i = lax.top_k(s, PEER_TOPK)
    cand_s = (top_s[..., 0, :, None] + top_s[..., 1, None, :]).reshape(bsz, seq, PEER_HEADS, PEER_TOPK * PEER_TOPK)
    cand_i = (top_i[..., 0, :, None] * PEER_N_KEYS + top_i[..., 1, None, :]).reshape(bsz, seq, PEER_HEADS, PEER_TOPK * PEER_TOPK)
    best_s, best_j = lax.top_k(cand_s, PEER_TOPK)
    idx = jnp.take_along_axis(cand_i, best_j, axis=-1)
    gate = jax.nn.softmax(best_s, axis=-1).astype(x.dtype)
    nb = seq // PEER_BLOCK

    def blockify(t):
        return t.reshape(bsz, nb, PEER_BLOCK, *t.shape[2:]).swapaxes(0, 1)

    def one_block(args):
        xb, ib, gb = args
        u = jnp.take(peer_u, ib, axis=0)
        hid = jnp.einsum("bqhkd,bqd->bqhk", u, xb)
        w = gb * jax.nn.gelu(hid, approximate=False)
        vv = jnp.take(peer_v, ib, axis=0)
        return jnp.einsum("bqhk,bqhkd->bqd", w, vv)

    y = lax.map(one_block, (blockify(x), blockify(idx), blockify(gate)))
    return y.swapaxes(0, 1).reshape(bsz, seq, D_MODEL)


def setup_inputs(seed: int = 0) -> dict:
    key = jax.random.key(seed)
    ks = iter(jax.random.split(key, 40))

    def nrm(shape, std):
        return jax.random.normal(next(ks), shape, jnp.float32) * std

    def near_one(shape):
        return 1.0 + 0.01 * jax.random.normal(next(ks), shape, jnp.float32)

    dt0 = jnp.exp(jax.random.uniform(next(ks), (DEPTH, SSD_HEADS), jnp.float32,
                                     math.log(1e-3), math.log(1e-1)))
    dt_bias = dt0 + jnp.log(-jnp.expm1(-dt0))
    a_log = jnp.log(jax.random.uniform(next(ks), (DEPTH, SSD_HEADS), jnp.float32, 1.0, 16.0))
    return {
        "x": nrm((BATCH, SEQ, D_MODEL), 1.0),
        "mem": nrm((BATCH, MEM_LEN, D_MODEL), 1.0),
        "w_in": nrm((DEPTH, D_MODEL, D_IN_PROJ), D_MODEL ** -0.5),
        "conv_w": nrm((DEPTH, CONV_WIDTH, D_XBC), CONV_WIDTH ** -0.5),
        "conv_b": nrm((DEPTH, D_XBC), 0.01),
        "dt_bias": dt_bias,
        "a_log": a_log,
        "d_skip": near_one((DEPTH, SSD_HEADS)),
        "ssd_norm_w": near_one((DEPTH, D_INNER)),
        "w_ssd_br": nrm((DEPTH, D_INNER, D_MODEL), D_INNER ** -0.5),
        "lam_q": nrm((DEPTH, 2, DIFF_HEAD_DIM), 0.1),
        "lam_k": nrm((DEPTH, 2, DIFF_HEAD_DIM), 0.1),
        "subln_w": near_one((DEPTH, DIFF_V_DIM)),
        "w_diff_br": nrm((DEPTH, DIFF_V, D_MODEL), DIFF_V ** -0.5),
        "gate_bias": nrm((DEPTH, 2, D_MODEL), 0.01),
        "w_o": nrm((DEPTH, D_MODEL, D_MODEL), BETA * D_MODEL ** -0.5),
        "ln1_g": near_one((DEPTH, D_MODEL)),
        "ln1_b": nrm((DEPTH, D_MODEL), 0.01),
        "w_cq": nrm((DEPTH, D_MODEL, D_MODEL), D_MODEL ** -0.5),
        "w_ck": nrm((DEPTH, D_MODEL, D_MODEL), D_MODEL ** -0.5),
        "w_cv": nrm((DEPTH, D_MODEL, D_MODEL), D_MODEL ** -0.5),
        "w_co": nrm((DEPTH, D_MODEL, D_MODEL), BETA * D_MODEL ** -0.5),
        "ln2_g": near_one((DEPTH, D_MODEL)),
        "ln2_b": nrm((DEPTH, D_MODEL), 0.01),
        "w_pq": nrm((DEPTH, D_MODEL, PEER_HEADS * PEER_D_KEY), D_MODEL ** -0.5),
        "sub_keys": nrm((DEPTH, PEER_HEADS, 2, PEER_N_KEYS, PEER_HALF), PEER_HALF ** -0.5),
        "peer_u": nrm((DEPTH, PEER_N_EXPERTS, D_MODEL), D_MODEL ** -0.5),
        "peer_v": nrm((DEPTH, PEER_N_EXPERTS, D_MODEL), BETA * PEER_HEADS ** -0.5),
        "ln3_g": near_one((DEPTH, D_MODEL)),
        "ln3_b": nrm((DEPTH, D_MODEL), 0.01),
    }


def reference(x, mem, w_in, conv_w, conv_b, dt_bias, a_log, d_skip, ssd_norm_w, w_ssd_br,
              lam_q, lam_k, subln_w, w_diff_br, gate_bias, w_o, ln1_g, ln1_b,
              w_cq, w_ck, w_cv, w_co, ln2_g, ln2_b, w_pq, sub_keys, peer_u, peer_v,
              ln3_g, ln3_b):
    for l in range(DEPTH):
        lambda_init = 0.8 - 0.6 * math.exp(-0.3 * l)
        h = x @ w_in[l]
        z, xbc, dt_raw, q, k, v, g_ssd, g_att = _split_cols(h)
        y_ssd = mamba2_branch(z, xbc, dt_raw, conv_w[l], conv_b[l], dt_bias[l], a_log[l],
                              d_skip[l], ssd_norm_w[l]) @ w_ssd_br[l]
        lq = lam_q[l].astype(jnp.float32)
        lk = lam_k[l].astype(jnp.float32)
        lam = jnp.exp(jnp.sum(lq[0] * lk[0])) - jnp.exp(jnp.sum(lq[1] * lk[1])) + lambda_init
        y_att = diff_attention(q, k, v, lam, lambda_init, subln_w[l]) @ w_diff_br[l]
        gate_a = jax.nn.sigmoid(g_ssd + gate_bias[l, 0])
        gate_b = jax.nn.sigmoid(g_att + gate_bias[l, 1])
        mix = (gate_a * y_ssd + gate_b * y_att) @ w_o[l]
        x = layer_norm(ALPHA * x + mix, ln1_g[l], ln1_b[l])
        x = layer_norm(ALPHA * x + memory_cross_attention(x, mem, w_cq[l], w_ck[l], w_cv[l], w_co[l]),
                       ln2_g[l], ln2_b[l])
        x = layer_norm(ALPHA * x + peer_ffn(x, w_pq[l], sub_keys[l], peer_u[l], peer_v[l]),
                       ln3_g[l], ln3_b[l])
    return x
```

```python
import functools
import math

import jax
import jax.numpy as jnp
from jax import lax
from jax.experimental import pallas as pl
from jax.experimental.pallas import tpu as pltpu

F32 = jnp.float32
BF16 = jnp.bfloat16

D_MODEL = 1024
D_INNER = 2048
SSD_HEADS = 32
SSD_HEAD_DIM = 64
SSD_GROUPS = 4
D_STATE = 128
CONV_WIDTH = 4
SSD_CHUNK = 128
D_XBC = D_INNER + 2 * SSD_GROUPS * D_STATE
DIFF_HEADS = 8
DIFF_HEAD_DIM = 64
DIFF_V_DIM = 128
RMS_EPS = 1e-5
LN_EPS = 1e-5
LANES = 128
VMEM_LIMIT = 56 * 1024 * 1024

_NT = (((1,), (1,)), ((), ()))
_TN = (((0,), (0,)), ((), ()))


def _cparams(*sem):
    return pltpu.CompilerParams(dimension_semantics=sem, vmem_limit_bytes=VMEM_LIMIT)


def _mm_kernel(a_ref, w_ref, o_ref):
    o_ref[...] = jnp.dot(a_ref[...], w_ref[...], preferred_element_type=F32).astype(o_ref.dtype)


def _matmul(a, w, out_dtype, tm, tn):
    m, k = a.shape
    n = w.shape[1]
    assert m % tm == 0 and n % tn == 0
    return pl.pallas_call(
        _mm_kernel,
        grid=(n // tn, m // tm),
        in_specs=[pl.BlockSpec((tm, k), lambda j, i: (i, 0)), pl.BlockSpec((k, tn), lambda j, i: (0, j))],
        out_specs=pl.BlockSpec((tm, tn), lambda j, i: (i, j)),
        out_shape=jax.ShapeDtypeStruct((m, n), out_dtype),
        compiler_params=_cparams("arbitrary", "arbitrary"),
        name="matmul",
    )(a, w)


def _silu(x):
    return x * (1.0 / (1.0 + jnp.exp(-x)))


def _conv_kernel(tiles_per_seq, x_ref, halo_ref, w_ref, b_ref, o_ref):
    ts = x_ref.shape[0]
    first = (pl.program_id(0) % tiles_per_seq) == 0
    halo = jnp.where(first, 0.0, halo_ref[...])
    xc = jnp.concatenate([halo, x_ref[...]], axis=0)
    acc = b_ref[...] + w_ref[CONV_WIDTH - 1:CONV_WIDTH, :] * x_ref[...]
    for kk in range(CONV_WIDTH - 1):
        off = 8 - (CONV_WIDTH - 1) + kk
        acc = acc + w_ref[kk:kk + 1, :] * xc[off:off + ts, :]
    o_ref[...] = _silu(acc)


def _conv_silu(x, w, b, seq, ts=512, tc=1024):
    t, c = x.shape
    assert seq % ts == 0 and c % tc == 0 and ts % 8 == 0
    hb = ts // 8
    return pl.pallas_call(
        functools.partial(_conv_kernel, seq // ts),
        grid=(t // ts, c // tc),
        in_specs=[
            pl.BlockSpec((ts, tc), lambda i, j: (i, j)),
            pl.BlockSpec((8, tc), lambda i, j: (jnp.maximum(i * hb - 1, 0), j)),
            pl.BlockSpec((CONV_WIDTH, tc), lambda i, j: (0, j)),
            pl.BlockSpec((1, tc), lambda i, j: (0, j)),
        ],
        out_specs=pl.BlockSpec((ts, tc), lambda i, j: (i, j)),
        out_shape=jax.ShapeDtypeStruct((t, c), F32),
        compiler_params=_cparams("arbitrary", "arbitrary"),
        name="conv_silu",
    )(x, x, w, b.reshape(1, c))


def _split3(v):
    hi = v.astype(BF16)
    r1 = v - hi.astype(F32)
    mid = r1.astype(BF16)
    lo = (r1 - mid.astype(F32)).astype(BF16)
    return jnp.concatenate([hi, mid, lo], axis=1)


def _ssd_kernel(xbc_ref, z_ref, dt_ref, dtb_ref, alog_ref, dskip_ref, e3_ref, nw_ref, o_ref, state_ref):
    L = SSD_CHUNK
    gw = D_INNER // SSD_GROUPS

    @pl.when(pl.program_id(1) == 0)
    def _():
        state_ref[...] = jnp.zeros_like(state_ref)

    x_dt = dt_ref[...] + dtb_ref[...]
    dt = jnp.maximum(x_dt, 0.0) + jnp.log1p(jnp.exp(-jnp.abs(x_dt)))
    a_neg = -jnp.exp(alog_ref[...])
    A = dt * a_neg
    row = lax.broadcasted_iota(jnp.int32, (L, L), 0)
    col = lax.broadcasted_iota(jnp.int32, (L, L), 1)
    tril = col <= row
    acs = jnp.dot(tril.astype(F32), A, precision=lax.Precision.HIGHEST, preferred_element_type=F32)
    a_last = acs[L - 1:L, :]
    w_state = dt * jnp.exp(a_last - acs)
    exp_a = jnp.exp(acs)
    cd = jnp.exp(jnp.broadcast_to(a_last, (8, LANES)))
    e3 = e3_ref[...]
    w_x = jnp.dot(_split3(w_state), e3, preferred_element_type=F32)
    expa_x = jnp.dot(_split3(exp_a), e3, preferred_element_type=F32)
    cd_x = jnp.dot(_split3(cd), e3, preferred_element_type=F32)[0:1, :]
    acs_t = acs.T
    dt_t = dt.T
    lane = lax.broadcasted_iota(jnp.int32, (L, LANES), 1)
    lo_half = lane < SSD_HEAD_DIM

    for g in range(SSD_GROUPS):
        bg = xbc_ref[:, D_INNER + g * D_STATE:D_INNER + (g + 1) * D_STATE].astype(BF16)
        cg = xbc_ref[:, D_INNER + SSD_GROUPS * D_STATE + g * D_STATE:D_INNER + SSD_GROUPS * D_STATE + (g + 1) * D_STATE].astype(BF16)
        xg = xbc_ref[:, g * gw:(g + 1) * gw]
        cb = lax.dot_general(cg, bg, _NT, preferred_element_type=F32)
        sg = state_ref[g]
        y_off = jnp.dot(cg, sg.astype(BF16), preferred_element_type=F32) * expa_x[:, g * gw:(g + 1) * gw]
        xw = (xg * w_x[:, g * gw:(g + 1) * gw]).astype(BF16)
        new_state = lax.dot_general(bg, xw, _TN, preferred_element_type=F32)
        state_ref[g] = sg * cd_x[:, g * gw:(g + 1) * gw] + new_state
        y_pairs = []
        for j in range(gw // LANES):
            ms = []
            for hh in range(2):
                h = g * (gw // SSD_HEAD_DIM) + 2 * j + hh
                seg = acs[:, h:h + 1] - acs_t[h:h + 1, :]
                ms.append(cb * jnp.exp(jnp.where(tril, seg, -jnp.inf)) * dt_t[h:h + 1, :])
            lhs = jnp.concatenate(ms, axis=1).astype(BF16)
            xp = xg[:, j * LANES:(j + 1) * LANES]
            rhs = jnp.concatenate([jnp.where(lo_half, xp, 0.0), jnp.where(lo_half, 0.0, xp)], axis=0).astype(BF16)
            y_pairs.append(jnp.dot(lhs, rhs, preferred_element_type=F32))
        y = jnp.concatenate(y_pairs, axis=1) + y_off + dskip_ref[:, g * gw:(g + 1) * gw] * xg
        y = y * _silu(z_ref[:, g * gw:(g + 1) * gw])
        ms_ = jnp.mean(y * y, axis=1, keepdims=True)
        y = y * lax.rsqrt(ms_ + RMS_EPS) * nw_ref[:, g * gw:(g + 1) * gw]
        o_ref[:, g * gw:(g + 1) * gw] = y.astype(o_ref.dtype)


def _head_expand_matrix():
    head_of_lane = jnp.arange(D_INNER) // SSD_HEAD_DIM
    e = (jnp.arange(LANES)[:, None] == head_of_lane[None, :]).astype(BF16)
    return jnp.concatenate([e, e, e], axis=0)


def _pad_lanes(v, fill=0.0):
    return jnp.pad(v.astype(F32), (0, LANES - v.shape[0]), constant_values=fill).reshape(1, LANES)


def _ssd(xbc_act, z, dt_raw, dt_bias, a_log, d_skip, norm_w, batch, seq):
    t = xbc_act.shape[0]
    nchunk = seq // SSD_CHUNK
    row = lambda b, c: (b * nchunk + c, 0)
    const = lambda b, c: (0, 0)
    dskip_x = jnp.repeat(d_skip.astype(F32), SSD_HEAD_DIM).reshape(1, D_INNER)
    return pl.pallas_call(
        _ssd_kernel,
        grid=(batch, nchunk),
        in_specs=[
            pl.BlockSpec((SSD_CHUNK, D_XBC), row),
            pl.BlockSpec((SSD_CHUNK, D_INNER), row),
            pl.BlockSpec((SSD_CHUNK, LANES), row),
            pl.BlockSpec((1, LANES), const),
            pl.BlockSpec((1, LANES), const),
            pl.BlockSpec((1, D_INNER), const),
            pl.BlockSpec((3 * LANES, D_INNER), const),
            pl.BlockSpec((1, D_INNER), const),
        ],
        out_specs=pl.BlockSpec((SSD_CHUNK, D_INNER), row),
        out_shape=jax.ShapeDtypeStruct((t, D_INNER), BF16),
        scratch_shapes=[pltpu.VMEM((SSD_GROUPS, D_STATE, D_INNER // SSD_GROUPS), F32)],
        compiler_params=_cparams("arbitrary", "arbitrary"),
        name="ssd",
    )(xbc_act, z, dt_raw, _pad_lanes(dt_bias), _pad_lanes(a_log), dskip_x, _head_expand_matrix(),
      norm_w.astype(F32).reshape(1, D_INNER))


def _attn_kernel(lambda_init, q_ref, k_ref, v_ref, lq_ref, lk_ref, sw_ref, o_ref, m_ref, l_ref, acc_ref):
    tq = q_ref.shape[0]
    qi = pl.program_id(2)
    lane = lax.broadcasted_iota(jnp.int32, (tq, LANES), 1)
    q = q_ref[...] * (DIFF_HEAD_DIM ** -0.5)
    zero = jnp.zeros_like(q)
    q2 = jnp.concatenate([jnp.where(lane < DIFF_HEAD_DIM, q, zero), jnp.where(lane < DIFF_HEAD_DIM, zero, q)], axis=0)
    m_ref[...] = jnp.full_like(m_ref, -jnp.inf)
    l_ref[...] = jnp.zeros_like(l_ref)
    acc_ref[...] = jnp.zeros_like(acc_ref)

    def step(kj, masked):
        start = pl.multiple_of(kj * tq, tq)
        k = k_ref[pl.ds(start, tq), :]
        v = v_ref[pl.ds(start, tq), :]
        s = lax.dot_general(k, q2, _NT, preferred_element_type=F32)
        if masked:
            kpos = lax.broadcasted_iota(jnp.int32, s.shape, 0)
            qpos = lax.broadcasted_iota(jnp.int32, s.shape, 1) & (tq - 1)
            s = jnp.where(kpos <= qpos, s, -jnp.inf)
        m_old = m_ref[...]
        m_new = jnp.maximum(m_old, jnp.max(s, axis=0, keepdims=True))
        alpha = jnp.exp(m_old - m_new)
        p = jnp.exp(s - m_new)
        l_ref[...] = alpha * l_ref[...] + jnp.sum(p, axis=0, keepdims=True)
        acc_ref[...] = acc_ref[...] * alpha + lax.dot_general(v, p.astype(BF16), _TN, preferred_element_type=F32)
        m_ref[...] = m_new

    def body(kj, carry):
        step(kj, False)
        return carry

    lax.fori_loop(0, qi, body, 0)
    step(qi, True)

    o_all = acc_ref[...] / l_ref[...]
    lam = (jnp.exp(jnp.sum(lq_ref[0:1, :] * lk_ref[0:1, :], axis=1, keepdims=True))
           - jnp.exp(jnp.sum(lq_ref[1:2, :] * lk_ref[1:2, :], axis=1, keepdims=True)) + lambda_init)
    o = o_all[:, :tq] - lam * o_all[:, tq:]
    ms = jnp.mean(o * o, axis=0, keepdims=True)
    o = o * lax.rsqrt(ms + RMS_EPS) * sw_ref[...] * (1.0 - lambda_init)
    o_ref[...] = o.T.astype(o_ref.dtype)


def _diff_attention(qkv, lam_q, lam_k, subln_w, lambda_init, batch, seq, tq=256):
    t = qkv.shape[0]
    nq = seq // tq
    nh = DIFF_HEADS
    assert tq & (tq - 1) == 0 and seq % tq == 0
    return pl.pallas_call(
        functools.partial(_attn_kernel, lambda_init),
        grid=(batch, nh, nq),
        in_specs=[
            pl.BlockSpec((tq, LANES), lambda b, h, i: (b * nq + i, h)),
            pl.BlockSpec((seq, LANES), lambda b, h, i: (b, nh + h)),
            pl.BlockSpec((seq, LANES), lambda b, h, i: (b, 2 * nh + h)),
            pl.BlockSpec((2, DIFF_HEAD_DIM), lambda b, h, i: (0, 0)),
            pl.BlockSpec((2, DIFF_HEAD_DIM), lambda b, h, i: (0, 0)),
            pl.BlockSpec((DIFF_V_DIM, 1), lambda b, h, i: (0, 0)),
        ],
        out_specs=pl.BlockSpec((tq, LANES), lambda b, h, i: (b * nq + i, h)),
        out_shape=jax.ShapeDtypeStruct((t, nh * DIFF_V_DIM), BF16),
        scratch_shapes=[pltpu.VMEM((1, 2 * tq), F32), pltpu.VMEM((1, 2 * tq), F32), pltpu.VMEM((DIFF_V_DIM, 2 * tq), F32)],
        compiler_params=_cparams("arbitrary", "arbitrary", "arbitrary"),
        name="diff_attention",
    )(qkv, qkv, qkv, lam_q.astype(F32), lam_k.astype(F32), subln_w.astype(F32).reshape(DIFF_V_DIM, 1))


def _layer_norm(v, g, b):
    mu = jnp.mean(v, axis=-1, keepdims=True)
    d = v - mu
    var = jnp.mean(d * d, axis=-1, keepdims=True)
    return d * lax.rsqrt(var + LN_EPS) * g + b


def _sigmoid(v):
    return 1.0 / (1.0 + jnp.exp(-v))


def _mix_kernel(alpha, ys_ref, oa_ref, gates_ref, x_ref, wssd_ref, wdiff_ref, wo_ref, gb_ref, g_ref, b_ref, o_ref):
    y_ssd = jnp.dot(ys_ref[...], wssd_ref[...], preferred_element_type=F32)
    y_att = jnp.dot(oa_ref[...], wdiff_ref[...], preferred_element_type=F32)
    ga = _sigmoid(gates_ref[:, :D_MODEL] + gb_ref[0:1, :])
    gb = _sigmoid(gates_ref[:, D_MODEL:] + gb_ref[1:2, :])
    mix = jnp.dot((ga * y_ssd + gb * y_att).astype(BF16), wo_ref[...], preferred_element_type=F32)
    o_ref[...] = _layer_norm(alpha * x_ref[...] + mix, g_ref[...], b_ref[...])


def _resident(shape):
    return pl.BlockSpec(shape, lambda *_: (0,) * len(shape))


def _mix(ys, oa, gates, x, w_ssd, w_diff, w_o, gate_bias, ln_g, ln_b, alpha, tm=512):
    t = x.shape[0]
    rows = lambda w: pl.BlockSpec((tm, w), lambda i: (i, 0))
    return pl.pallas_call(
        functools.partial(_mix_kernel, alpha),
        grid=(t // tm,),
        in_specs=[rows(D_INNER), rows(D_MODEL), rows(2 * D_MODEL), rows(D_MODEL),
                  _resident((D_INNER, D_MODEL)), _resident((D_MODEL, D_MODEL)), _resident((D_MODEL, D_MODEL)),
                  _resident((2, D_MODEL)), _resident((1, D_MODEL)), _resident((1, D_MODEL))],
        out_specs=rows(D_MODEL),
        out_shape=jax.ShapeDtypeStruct((t, D_MODEL), F32),
        compiler_params=_cparams("arbitrary"),
        name="mix_ln1",
    )(ys, oa, gates, x, w_ssd, w_diff, w_o, gate_bias.astype(F32), ln_g.reshape(1, -1), ln_b.reshape(1, -1))


def _cross_kernel(alpha, nheads, x_ref, kv_ref, wq_ref, wo_ref, g_ref, b_ref, o_ref):
    x = x_ref[...]
    hd = D_MODEL // nheads
    q = jnp.dot(x.astype(BF16), wq_ref[...], preferred_element_type=F32).astype(BF16)
    outs = []
    for h in range(nheads):
        k = kv_ref[:, h * hd:(h + 1) * hd]
        v = kv_ref[:, D_MODEL + h * hd:D_MODEL + (h + 1) * hd]
        s = lax.dot_general(q[:, h * hd:(h + 1) * hd], k, _NT, preferred_element_type=F32) * (hd ** -0.5)
        e = jnp.exp(s - jnp.max(s, axis=-1, keepdims=True))
        p = e / jnp.sum(e, axis=-1, keepdims=True)
        outs.append(jnp.dot(p.astype(BF16), v, preferred_element_type=F32))
    o = jnp.concatenate(outs, axis=1).astype(BF16)
    y = jnp.dot(o, wo_ref[...], preferred_element_type=F32)
    o_ref[...] = _layer_norm(alpha * x + y, g_ref[...], b_ref[...])


def _cross_attention(x, kv, w_cq, w_co, ln_g, ln_b, alpha, nheads, batch, seq, tm=512):
    t = x.shape[0]
    mlen = kv.shape[0] // batch
    per = seq // tm
    return pl.pallas_call(
        functools.partial(_cross_kernel, alpha, nheads),
        grid=(t // tm,),
        in_specs=[pl.BlockSpec((tm, D_MODEL), lambda i: (i, 0)),
                  pl.BlockSpec((mlen, 2 * D_MODEL), lambda i: (i // per, 0)),
                  _resident((D_MODEL, D_MODEL)), _resident((D_MODEL, D_MODEL)),
                  _resident((1, D_MODEL)), _resident((1, D_MODEL))],
        out_specs=pl.BlockSpec((tm, D_MODEL), lambda i: (i, 0)),
        out_shape=jax.ShapeDtypeStruct((t, D_MODEL), F32),
        compiler_params=_cparams("arbitrary"),
        name="cross_ln2",
    )(x, kv, w_cq, w_co, ln_g.reshape(1, -1), ln_b.reshape(1, -1))


PEER_HEADS = 8
PEER_N_KEYS = 128
PEER_TOPK = 16
PEER_HALF = 128


def _topk_rows(s, nrows):
    rid = lax.broadcasted_iota(jnp.int32, s.shape, 0).astype(F32)
    vals, idxs = [], []
    for _ in range(PEER_TOPK):
        mx = jnp.max(s, axis=0, keepdims=True)
        idx = jnp.min(jnp.where(s == mx, rid, float(nrows)), axis=0, keepdims=True)
        s = jnp.where(rid == idx, -jnp.inf, s)
        vals.append(mx)
        idxs.append(idx)
    return jnp.concatenate(vals, axis=0), jnp.concatenate(idxs, axis=0)


def _pick_rows(sel, table):
    out = jnp.zeros_like(sel)
    for a in range(PEER_TOPK):
        out = jnp.where(sel == float(a), table[a:a + 1, :], out)
    return out


def _peer_route_kernel(x_ref, wq_ref, sk_ref, i_ref, j_ref, g_ref, st_ref, tv_ref, ti_ref, it_ref, jt_ref, gt_ref):
    tm = x_ref.shape[0]
    nhh = 2 * PEER_HEADS
    q = jnp.dot(x_ref[...].astype(BF16), wq_ref[...], preferred_element_type=F32).astype(BF16)
    for hh in range(nhh):
        st_ref[hh] = lax.dot_general(sk_ref[hh], q[:, hh * PEER_HALF:(hh + 1) * PEER_HALF], _NT,
                                     preferred_element_type=F32)
    ngrp = tm // LANES

    def stage1(n, carry):
        hh = n // ngrp
        col = pl.multiple_of((n % ngrp) * LANES, LANES)
        vals, idxs = _topk_rows(st_ref[hh, :, pl.ds(col, LANES)], PEER_N_KEYS)
        tv_ref[hh, :, pl.ds(col, LANES)] = vals
        ti_ref[hh, :, pl.ds(col, LANES)] = idxs
        return carry

    lax.fori_loop(0, nhh * ngrp, stage1, 0)

    def stage2(n, carry):
        h = n // ngrp
        col = pl.multiple_of((n % ngrp) * LANES, LANES)
        v0 = tv_ref[2 * h, :, pl.ds(col, LANES)]
        v1 = tv_ref[2 * h + 1, :, pl.ds(col, LANES)]
        cand = jnp.concatenate([v0[a:a + 1, :] + v1 for a in range(PEER_TOPK)], axis=0)
        best, flat = _topk_rows(cand, PEER_TOPK * PEER_TOPK)
        a_sel = jnp.floor(flat * (1.0 / PEER_TOPK))
        b_sel = flat - a_sel * PEER_TOPK
        ii = _pick_rows(a_sel, ti_ref[2 * h, :, pl.ds(col, LANES)])
        jj = _pick_rows(b_sel, ti_ref[2 * h + 1, :, pl.ds(col, LANES)])
        e = jnp.exp(best - best[0:1, :])
        gate = e / jnp.sum(e, axis=0, keepdims=True)
        r0 = pl.multiple_of(h * PEER_TOPK, PEER_TOPK)
        it_ref[pl.ds(r0, PEER_TOPK), pl.ds(col, LANES)] = ii
        jt_ref[pl.ds(r0, PEER_TOPK), pl.ds(col, LANES)] = jj
        gt_ref[pl.ds(r0, PEER_TOPK), pl.ds(col, LANES)] = gate
        return carry

    lax.fori_loop(0, PEER_HEADS * ngrp, stage2, 0)
    i_ref[...] = it_ref[...].T.astype(jnp.int32)
    j_ref[...] = jt_ref[...].T.astype(jnp.int32)
    g_ref[...] = gt_ref[...].T


def _peer_route(x, w_pq, sub_keys, tm=256):
    t = x.shape[0]
    nhh = 2 * PEER_HEADS
    out = pl.BlockSpec((tm, LANES), lambda i: (i, 0))
    return pl.pallas_call(
        _peer_route_kernel,
        grid=(t // tm,),
        in_specs=[pl.BlockSpec((tm, D_MODEL), lambda i: (i, 0)),
                  _resident((D_MODEL, nhh * PEER_HALF)),
                  _resident((nhh, PEER_N_KEYS, PEER_HALF))],
        out_specs=[out, out, out],
        out_shape=[jax.ShapeDtypeStruct((t, LANES), jnp.int32), jax.ShapeDtypeStruct((t, LANES), jnp.int32),
                   jax.ShapeDtypeStruct((t, LANES), F32)],
        scratch_shapes=[pltpu.VMEM((nhh, PEER_N_KEYS, tm), F32), pltpu.VMEM((nhh, PEER_TOPK, tm), F32),
                        pltpu.VMEM((nhh, PEER_TOPK, tm), F32), pltpu.VMEM((LANES, tm), F32),
                        pltpu.VMEM((LANES, tm), F32), pltpu.VMEM((LANES, tm), F32)],
        compiler_params=_cparams("arbitrary"),
        name="peer_route",
    )(x, w_pq, sub_keys)


WG_PITCH = 136


def _peer_expert_kernel(alpha, x_ref, i_ref, j_ref, g_ref, u_ref, v_ref, lg_ref, lb_ref, o_ref, wg_ref, acc_ref, xb_ref):
    tm = x_ref.shape[0]
    ec = u_ref.shape[0]
    jc = pl.program_id(1)

    @pl.when(jc == 0)
    def _():
        xb_ref[...] = x_ref[...].astype(BF16)
        acc_ref[...] = jnp.zeros_like(acc_ref)
        rid = lax.broadcasted_iota(jnp.int32, (PEER_N_KEYS, LANES), 0)

        def token(t, carry):
            irow = i_ref[pl.ds(t, 1), :]
            jrow = j_ref[pl.ds(t, 1), :]
            g = g_ref[pl.ds(t, 1), :]
            g_hi = g.astype(BF16).astype(F32)
            g_lo = g - g_hi
            a1 = jnp.where(rid == irow, 1.0, 0.0).astype(BF16)
            hit = rid == jrow
            lhs = jnp.concatenate([a1, a1], axis=1)
            rhs = jnp.concatenate([jnp.where(hit, g_hi, 0.0).astype(BF16), jnp.where(hit, g_lo, 0.0).astype(BF16)], axis=1)
            w = lax.dot_general(lhs, rhs, _NT, preferred_element_type=F32)
            wg_ref[pl.ds(pl.multiple_of(t * WG_PITCH, 8), PEER_N_KEYS), :] = w
            return carry

        lax.fori_loop(0, tm, token, 0)

    hid = lax.dot_general(xb_ref[...], u_ref[...], _NT, preferred_element_type=F32)
    act = 0.5 * hid * (1.0 + lax.erf(hid * (2.0 ** -0.5)))
    nblk = ec // LANES
    parts = []
    for cc in range(nblk):
        wrow = wg_ref[pl.ds(jc * nblk + cc, tm, stride=WG_PITCH), :]
        parts.append((wrow * act[:, cc * LANES:(cc + 1) * LANES]).astype(BF16))
    acc_ref[...] += jnp.dot(jnp.concatenate(parts, axis=1), v_ref[...], preferred_element_type=F32)

    @pl.when(jc == pl.num_programs(1) - 1)
    def _():
        o_ref[...] = _layer_norm(alpha * x_ref[...] + acc_ref[...], lg_ref[...], lb_ref[...])


def _peer_expert(x, ii, jj, gg, u, v, ln_g, ln_b, alpha, tm=256, ec=1024):
    t = x.shape[0]
    ne = u.shape[0]
    tok = lambda w: pl.BlockSpec((tm, w), lambda i, j: (i, 0))
    return pl.pallas_call(
        functools.partial(_peer_expert_kernel, alpha),
        grid=(t // tm, ne // ec),
        in_specs=[tok(D_MODEL), tok(LANES), tok(LANES), tok(LANES),
                  pl.BlockSpec((ec, D_MODEL), lambda i, j: (j, 0)), pl.BlockSpec((ec, D_MODEL), lambda i, j: (j, 0)),
                  _resident((1, D_MODEL)), _resident((1, D_MODEL))],
        out_specs=tok(D_MODEL),
        out_shape=jax.ShapeDtypeStruct((t, D_MODEL), F32),
        scratch_shapes=[pltpu.VMEM((tm * WG_PITCH, LANES), F32), pltpu.VMEM((tm, D_MODEL), F32),
                        pltpu.VMEM((tm, D_MODEL), BF16)],
        compiler_params=_cparams("arbitrary", "arbitrary"),
        name="peer_expert_ln3",
    )(x, ii, jj, gg, u, v, ln_g.reshape(1, -1), ln_b.reshape(1, -1))


def kernel(x, mem, w_in, conv_w, conv_b, dt_bias, a_log, d_skip, ssd_norm_w, w_ssd_br, lam_q, lam_k, subln_w, w_diff_br, gate_bias, w_o, ln1_g, ln1_b, w_cq, w_ck, w_cv, w_co, ln2_g, ln2_b, w_pq, sub_keys, peer_u, peer_v, ln3_g, ln3_b):
    batch, seq, d_model = x.shape
    depth = w_in.shape[0]
    t = batch * seq
    assert d_model == D_MODEL and w_in.shape[2] == D_INNER + D_XBC + SSD_HEADS + 5 * D_MODEL
    assert seq % 512 == 0 and sub_keys.shape[1:] == (PEER_HEADS, 2, PEER_N_KEYS, PEER_HALF)
    alpha = (2 * depth) ** 0.25
    mem_heads = 4
    cuts = [D_INNER, D_INNER + D_XBC, D_INNER + D_XBC + SSD_HEADS]
    xf = x.reshape(t, d_model)
    memf = mem.reshape(batch * mem.shape[1], d_model).astype(BF16)
    for l in range(depth):
        lambda_init = 0.8 - 0.6 * math.exp(-0.3 * l)
        w = w_in[l].astype(BF16)
        w_dt = jnp.pad(w[:, cuts[1]:cuts[2]], ((0, 0), (0, LANES - SSD_HEADS)))
        xb = xf.astype(BF16)
        z = _matmul(xb, w[:, :cuts[0]], F32, 512, 1024)
        xbc = _matmul(xb, w[:, cuts[0]:cuts[1]], F32, 512, 1024)
        dt_raw = _matmul(xb, w_dt, F32, 512, LANES)
        qkv = _matmul(xb, w[:, cuts[2]:cuts[2] + 3 * D_MODEL], BF16, 512, 1024)
        gates = _matmul(xb, w[:, cuts[2] + 3 * D_MODEL:], F32, 512, 1024)
        xbc_act = _conv_silu(xbc, conv_w[l].astype(F32), conv_b[l].astype(F32), seq)
        y_ssd = _ssd(xbc_act, z, dt_raw, dt_bias[l], a_log[l], d_skip[l], ssd_norm_w[l], batch, seq)
        o_att = _diff_attention(qkv, lam_q[l], lam_k[l], subln_w[l], lambda_init, batch, seq)
        xf = _mix(y_ssd, o_att, gates, xf, w_ssd_br[l].astype(BF16), w_diff_br[l].astype(BF16), w_o[l].astype(BF16),
                  gate_bias[l], ln1_g[l], ln1_b[l], alpha)
        kv = _matmul(memf, jnp.concatenate([w_ck[l], w_cv[l]], axis=1).astype(BF16), BF16, memf.shape[0], 1024)
        xf = _cross_attention(xf, kv, w_cq[l].astype(BF16), w_co[l].astype(BF16), ln2_g[l], ln2_b[l], alpha,
                              mem_heads, batch, seq)
        ii, jj, gg = _peer_route(xf, w_pq[l].astype(BF16),
                                 sub_keys[l].reshape(2 * PEER_HEADS, PEER_N_KEYS, PEER_HALF).astype(BF16))
        xf = _peer_expert(xf, ii, jj, gg, peer_u[l].astype(BF16), peer_v[l].astype(BF16), ln3_g[l], ln3_b[l], alpha)
    return xf.reshape(batch, seq, d_model)
```

```python
import functools
import math

import jax
import jax.numpy as jnp
from jax import lax
from jax.experimental import pallas as pl
from jax.experimental.pallas import tpu as pltpu

F32 = jnp.float32
BF16 = jnp.bfloat16

D_MODEL = 1024
D_INNER = 2048
SSD_HEADS = 32
SSD_HEAD_DIM = 64
SSD_GROUPS = 4
D_STATE = 128
CONV_WIDTH = 4
SSD_CHUNK = 128
D_XBC = D_INNER + 2 * SSD_GROUPS * D_STATE
DIFF_HEADS = 8
DIFF_HEAD_DIM = 64
DIFF_V_DIM = 128
RMS_EPS = 1e-5
LN_EPS = 1e-5
LANES = 128
VMEM_LIMIT = 56 * 1024 * 1024

_NT = (((1,), (1,)), ((), ()))
_TN = (((0,), (0,)), ((), ()))


def _cparams(*sem):
    return pltpu.CompilerParams(dimension_semantics=sem, vmem_limit_bytes=VMEM_LIMIT)


def _mm_kernel(a_ref, w_ref, o_ref):
    o_ref[...] = jnp.dot(a_ref[...], w_ref[...], preferred_element_type=F32).astype(o_ref.dtype)


def _mm_scaled_kernel(a_ref, w_ref, c_ref, o_ref):
    o_ref[...] = (jnp.dot(a_ref[...], w_ref[...], preferred_element_type=F32) * c_ref[...]).astype(o_ref.dtype)


def _matmul(a, w, out_dtype, tm, tn, col_scale=None):
    m, k = a.shape
    n = w.shape[1]
    assert m % tm == 0 and n % tn == 0
    in_specs = [pl.BlockSpec((tm, k), lambda j, i: (i, 0)), pl.BlockSpec((k, tn), lambda j, i: (0, j))]
    operands = [a, w]
    if col_scale is not None:
        in_specs.append(pl.BlockSpec((1, tn), lambda j, i: (0, j)))
        operands.append(col_scale.reshape(1, n))
    return pl.pallas_call(
        _mm_kernel if col_scale is None else _mm_scaled_kernel,
        grid=(n // tn, m // tm),
        in_specs=in_specs,
        out_specs=pl.BlockSpec((tm, tn), lambda j, i: (i, j)),
        out_shape=jax.ShapeDtypeStruct((m, n), out_dtype),
        compiler_params=_cparams("arbitrary", "arbitrary"),
        name="matmul",
    )(*operands)


def _silu(x):
    return x * (1.0 / (1.0 + jnp.exp(-x)))


def _conv_kernel(tiles_per_seq, x_ref, halo_ref, w_ref, b_ref, o_ref):
    ts = x_ref.shape[0]
    first = (pl.program_id(0) % tiles_per_seq) == 0
    halo = jnp.where(first, 0.0, halo_ref[...])
    xc = jnp.concatenate([halo, x_ref[...]], axis=0)
    acc = b_ref[...] + w_ref[CONV_WIDTH - 1:CONV_WIDTH, :] * x_ref[...]
    for kk in range(CONV_WIDTH - 1):
        off = 8 - (CONV_WIDTH - 1) + kk
        acc = acc + w_ref[kk:kk + 1, :] * xc[off:off + ts, :]
    o_ref[...] = _silu(acc)


def _conv_silu(x, w, b, seq, ts=512, tc=1024):
    t, c = x.shape
    assert seq % ts == 0 and c % tc == 0 and ts % 8 == 0
    hb = ts // 8
    return pl.pallas_call(
        functools.partial(_conv_kernel, seq // ts),
        grid=(t // ts, c // tc),
        in_specs=[
            pl.BlockSpec((ts, tc), lambda i, j: (i, j)),
            pl.BlockSpec((8, tc), lambda i, j: (jnp.maximum(i * hb - 1, 0), j)),
            pl.BlockSpec((CONV_WIDTH, tc), lambda i, j: (0, j)),
            pl.BlockSpec((1, tc), lambda i, j: (0, j)),
        ],
        out_specs=pl.BlockSpec((ts, tc), lambda i, j: (i, j)),
        out_shape=jax.ShapeDtypeStruct((t, c), F32),
        compiler_params=_cparams("arbitrary", "arbitrary"),
        name="conv_silu",
    )(x, x, w, b.reshape(1, c))


def _split3(v):
    hi = v.astype(BF16)
    r1 = v - hi.astype(F32)
    mid = r1.astype(BF16)
    lo = (r1 - mid.astype(F32)).astype(BF16)
    return jnp.concatenate([hi, mid, lo], axis=1)


def _ssd_kernel(xbc_ref, z_ref, dt_ref, dtb_ref, alog_ref, dskip_ref, e3_ref, nw_ref, o_ref, state_ref):
    L = SSD_CHUNK
    gw = D_INNER // SSD_GROUPS

    @pl.when(pl.program_id(1) == 0)
    def _():
        state_ref[...] = jnp.zeros_like(state_ref)

    x_dt = dt_ref[...] + dtb_ref[...]
    dt = jnp.maximum(x_dt, 0.0) + jnp.log1p(jnp.exp(-jnp.abs(x_dt)))
    a_neg = -jnp.exp(alog_ref[...])
    A = dt * a_neg
    row = lax.broadcasted_iota(jnp.int32, (L, L), 0)
    col = lax.broadcasted_iota(jnp.int32, (L, L), 1)
    tril = col <= row
    acs = jnp.dot(tril.astype(F32), A, precision=lax.Precision.HIGHEST, preferred_element_type=F32)
    a_last = acs[L - 1:L, :]
    w_state = dt * jnp.exp(a_last - acs)
    exp_a = jnp.exp(acs)
    cd = jnp.exp(jnp.broadcast_to(a_last, (8, LANES)))
    e3 = e3_ref[...]
    w_x = jnp.dot(_split3(w_state), e3, preferred_element_type=F32)
    expa_x = jnp.dot(_split3(exp_a), e3, preferred_element_type=F32)
    cd_x = jnp.dot(_split3(cd), e3, preferred_element_type=F32)[0:1, :]
    acs_t = acs.T
    dt_t = dt.T
    lane = lax.broadcasted_iota(jnp.int32, (L, LANES), 1)
    lo_half = lane < SSD_HEAD_DIM

    for g in range(SSD_GROUPS):
        bg = xbc_ref[:, D_INNER + g * D_STATE:D_INNER + (g + 1) * D_STATE].astype(BF16)
        cg = xbc_ref[:, D_INNER + SSD_GROUPS * D_STATE + g * D_STATE:D_INNER + SSD_GROUPS * D_STATE + (g + 1) * D_STATE].astype(BF16)
        xg = xbc_ref[:, g * gw:(g + 1) * gw]
        cb = lax.dot_general(cg, bg, _NT, preferred_element_type=F32)
        sg = state_ref[g]
        y_off = jnp.dot(cg, sg.astype(BF16), preferred_element_type=F32) * expa_x[:, g * gw:(g + 1) * gw]
        xw = (xg * w_x[:, g * gw:(g + 1) * gw]).astype(BF16)
        new_state = lax.dot_general(bg, xw, _TN, preferred_element_type=F32)
        state_ref[g] = sg * cd_x[:, g * gw:(g + 1) * gw] + new_state
        y_pairs = []
        for j in range(gw // LANES):
            ms = []
            for hh in range(2):
                h = g * (gw // SSD_HEAD_DIM) + 2 * j + hh
                seg = acs[:, h:h + 1] - acs_t[h:h + 1, :]
                ms.append(cb * jnp.exp(jnp.where(tril, seg, -jnp.inf)) * dt_t[h:h + 1, :])
            lhs = jnp.concatenate(ms, axis=1).astype(BF16)
            xp = xg[:, j * LANES:(j + 1) * LANES]
            rhs = jnp.concatenate([jnp.where(lo_half, xp, 0.0), jnp.where(lo_half, 0.0, xp)], axis=0).astype(BF16)
            y_pairs.append(jnp.dot(lhs, rhs, preferred_element_type=F32))
        y = jnp.concatenate(y_pairs, axis=1) + y_off + dskip_ref[:, g * gw:(g + 1) * gw] * xg
        y = y * _silu(z_ref[:, g * gw:(g + 1) * gw])
        ms_ = jnp.mean(y * y, axis=1, keepdims=True)
        y = y * lax.rsqrt(ms_ + RMS_EPS) * nw_ref[:, g * gw:(g + 1) * gw]
        o_ref[:, g * gw:(g + 1) * gw] = y.astype(o_ref.dtype)


def _head_expand_matrix():
    head_of_lane = jnp.arange(D_INNER) // SSD_HEAD_DIM
    e = (jnp.arange(LANES)[:, None] == head_of_lane[None, :]).astype(BF16)
    return jnp.concatenate([e, e, e], axis=0)


def _pad_lanes(v, fill=0.0):
    return jnp.pad(v.astype(F32), (0, LANES - v.shape[0]), constant_values=fill).reshape(1, LANES)


def _ssd(xbc_act, z, dt_raw, dt_bias, a_log, d_skip, norm_w, batch, seq):
    t = xbc_act.shape[0]
    nchunk = seq // SSD_CHUNK
    row = lambda b, c: (b * nchunk + c, 0)
    const = lambda b, c: (0, 0)
    dskip_x = jnp.repeat(d_skip.astype(F32), SSD_HEAD_DIM).reshape(1, D_INNER)
    return pl.pallas_call(
        _ssd_kernel,
        grid=(batch, nchunk),
        in_specs=[
            pl.BlockSpec((SSD_CHUNK, D_XBC), row),
            pl.BlockSpec((SSD_CHUNK, D_INNER), row),
            pl.BlockSpec((SSD_CHUNK, LANES), row),
            pl.BlockSpec((1, LANES), const),
            pl.BlockSpec((1, LANES), const),
            pl.BlockSpec((1, D_INNER), const),
            pl.BlockSpec((3 * LANES, D_INNER), const),
            pl.BlockSpec((1, D_INNER), const),
        ],
        out_specs=pl.BlockSpec((SSD_CHUNK, D_INNER), row),
        out_shape=jax.ShapeDtypeStruct((t, D_INNER), BF16),
        scratch_shapes=[pltpu.VMEM((SSD_GROUPS, D_STATE, D_INNER // SSD_GROUPS), F32)],
        compiler_params=_cparams("arbitrary", "arbitrary"),
        name="ssd",
    )(xbc_act, z, dt_raw, _pad_lanes(dt_bias), _pad_lanes(a_log), dskip_x, _head_expand_matrix(),
      norm_w.astype(F32).reshape(1, D_INNER))


def _attn_kernel(lambda_init, q_ref, k_ref, v_ref, lq_ref, lk_ref, sw_ref, o_ref, q2_ref, s_ref, smax_ref, m_ref, l_ref,
                 acc_ref):
    tq = q_ref.shape[0]
    qi = pl.program_id(2)
    lane = lax.broadcasted_iota(jnp.int32, (tq, LANES), 1)
    q = q_ref[...]
    zero = jnp.zeros_like(q)
    q2_ref[0:tq, :] = jnp.where(lane < DIFF_HEAD_DIM, q, zero)
    q2_ref[tq:2 * tq, :] = jnp.where(lane < DIFF_HEAD_DIM, zero, q)
    m_ref[...] = jnp.full_like(m_ref, -jnp.inf)
    l_ref[...] = jnp.zeros_like(l_ref)
    acc_ref[...] = jnp.zeros_like(acc_ref)

    def produce(kj, slot, keep_all):
        k = k_ref[pl.ds(pl.multiple_of(kj * tq, tq), tq), :]
        s_blk = lax.dot_general(k, q2_ref[...], _NT, preferred_element_type=F32)
        if keep_all is not True:
            kpos = lax.broadcasted_iota(jnp.int32, s_blk.shape, 0)
            qpos = lax.broadcasted_iota(jnp.int32, s_blk.shape, 1) & (tq - 1)
            s_blk = jnp.where((kpos <= qpos) | keep_all, s_blk, -jnp.inf)
        s_ref[slot] = s_blk
        smax_ref[slot] = jnp.max(s_blk, axis=0, keepdims=True)

    def consume(kj, slot):
        v = v_ref[pl.ds(pl.multiple_of(kj * tq, tq), tq), :]
        m_old = m_ref[...]
        m_new = jnp.maximum(m_old, smax_ref[slot])
        alpha = jnp.exp2(m_old - m_new)
        p = jnp.exp2(s_ref[slot] - m_new)
        l_ref[...] = alpha * l_ref[...] + jnp.sum(p, axis=0, keepdims=True)
        acc_ref[...] = acc_ref[...] * alpha + lax.dot_general(v, p.astype(BF16), _TN, preferred_element_type=F32)
        m_ref[...] = m_new

    def by_parity(idx, fn):
        lax.cond((idx & 1) == 0, lambda: fn(0), lambda: fn(1))

    produce(0, 0, qi > 0)

    def body(kj, carry):
        def step(cur):
            produce(kj + 1, 1 - cur, True)
            consume(kj, cur)
        by_parity(kj, step)
        return carry

    lax.fori_loop(0, qi - 1, body, 0)

    @pl.when(qi >= 1)
    def _():
        def step(cur):
            produce(qi, 1 - cur, False)
            consume(qi - 1, cur)
        by_parity(qi - 1, step)

    by_parity(qi, lambda cur: consume(qi, cur))

    o_all = acc_ref[...] / l_ref[...]
    lam = (jnp.exp(jnp.sum(lq_ref[0:1, :] * lk_ref[0:1, :], axis=1, keepdims=True))
           - jnp.exp(jnp.sum(lq_ref[1:2, :] * lk_ref[1:2, :], axis=1, keepdims=True)) + lambda_init)
    o = o_all[:, :tq] - lam * o_all[:, tq:]
    ms = jnp.mean(o * o, axis=0, keepdims=True)
    o = o * lax.rsqrt(ms + RMS_EPS) * sw_ref[...] * (1.0 - lambda_init)
    o_ref[...] = o.T.astype(o_ref.dtype)


def _diff_attention(qkv, lam_q, lam_k, subln_w, lambda_init, batch, seq, tq=512):
    t = qkv.shape[0]
    nq = seq // tq
    nh = DIFF_HEADS
    assert tq & (tq - 1) == 0 and seq % tq == 0
    return pl.pallas_call(
        functools.partial(_attn_kernel, lambda_init),
        grid=(batch, nh, nq),
        in_specs=[
            pl.BlockSpec((tq, LANES), lambda b, h, i: (b * nq + i, h)),
            pl.BlockSpec((seq, LANES), lambda b, h, i: (b, nh + h)),
            pl.BlockSpec((seq, LANES), lambda b, h, i: (b, 2 * nh + h)),
            pl.BlockSpec((2, DIFF_HEAD_DIM), lambda b, h, i: (0, 0)),
            pl.BlockSpec((2, DIFF_HEAD_DIM), lambda b, h, i: (0, 0)),
            pl.BlockSpec((DIFF_V_DIM, 1), lambda b, h, i: (0, 0)),
        ],
        out_specs=pl.BlockSpec((tq, LANES), lambda b, h, i: (b * nq + i, h)),
        out_shape=jax.ShapeDtypeStruct((t, nh * DIFF_V_DIM), BF16),
        scratch_shapes=[pltpu.VMEM((2 * tq, LANES), BF16), pltpu.VMEM((2, tq, 2 * tq), F32),
                        pltpu.VMEM((2, 1, 2 * tq), F32), pltpu.VMEM((1, 2 * tq), F32), pltpu.VMEM((1, 2 * tq), F32), pltpu.VMEM((DIFF_V_DIM, 2 * tq), F32)],
        compiler_params=_cparams("arbitrary", "arbitrary", "arbitrary"),
        name="diff_attention",
    )(qkv, qkv, qkv, lam_q.astype(F32), lam_k.astype(F32), subln_w.astype(F32).reshape(DIFF_V_DIM, 1))


def _layer_norm(v, g, b):
    mu = jnp.mean(v, axis=-1, keepdims=True)
    d = v - mu
    var = jnp.mean(d * d, axis=-1, keepdims=True)
    return d * lax.rsqrt(var + LN_EPS) * g + b


def _sigmoid(v):
    return 1.0 / (1.0 + jnp.exp(-v))


def _mix_kernel(alpha, ys_ref, oa_ref, gates_ref, x_ref, wssd_ref, wdiff_ref, wo_ref, gb_ref, g_ref, b_ref, o_ref):
    y_ssd = jnp.dot(ys_ref[...], wssd_ref[...], preferred_element_type=F32)
    y_att = jnp.dot(oa_ref[...], wdiff_ref[...], preferred_element_type=F32)
    ga = _sigmoid(gates_ref[:, :D_MODEL] + gb_ref[0:1, :])
    gb = _sigmoid(gates_ref[:, D_MODEL:] + gb_ref[1:2, :])
    mix = jnp.dot((ga * y_ssd + gb * y_att).astype(BF16), wo_ref[...], preferred_element_type=F32)
    o_ref[...] = _layer_norm(alpha * x_ref[...] + mix, g_ref[...], b_ref[...])


def _resident(shape):
    return pl.BlockSpec(shape, lambda *_: (0,) * len(shape))


def _mix(ys, oa, gates, x, w_ssd, w_diff, w_o, gate_bias, ln_g, ln_b, alpha, tm=512):
    t = x.shape[0]
    rows = lambda w: pl.BlockSpec((tm, w), lambda i: (i, 0))
    return pl.pallas_call(
        functools.partial(_mix_kernel, alpha),
        grid=(t // tm,),
        in_specs=[rows(D_INNER), rows(D_MODEL), rows(2 * D_MODEL), rows(D_MODEL),
                  _resident((D_INNER, D_MODEL)), _resident((D_MODEL, D_MODEL)), _resident((D_MODEL, D_MODEL)),
                  _resident((2, D_MODEL)), _resident((1, D_MODEL)), _resident((1, D_MODEL))],
        out_specs=rows(D_MODEL),
        out_shape=jax.ShapeDtypeStruct((t, D_MODEL), F32),
        compiler_params=_cparams("arbitrary"),
        name="mix_ln1",
    )(ys, oa, gates, x, w_ssd, w_diff, w_o, gate_bias.astype(F32), ln_g.reshape(1, -1), ln_b.reshape(1, -1))


def _cross_kernel(alpha, nheads, x_ref, kv_ref, wq_ref, wo_ref, g_ref, b_ref, o_ref):
    x = x_ref[...]
    hd = D_MODEL // nheads
    q = jnp.dot(x.astype(BF16), wq_ref[...], preferred_element_type=F32).astype(BF16)
    outs = []
    for h in range(nheads):
        k = kv_ref[:, h * hd:(h + 1) * hd]
        v = kv_ref[:, D_MODEL + h * hd:D_MODEL + (h + 1) * hd]
        s = lax.dot_general(q[:, h * hd:(h + 1) * hd], k, _NT, preferred_element_type=F32) * (hd ** -0.5)
        e = jnp.exp(s - jnp.max(s, axis=-1, keepdims=True))
        p = e / jnp.sum(e, axis=-1, keepdims=True)
        outs.append(jnp.dot(p.astype(BF16), v, preferred_element_type=F32))
    o = jnp.concatenate(outs, axis=1).astype(BF16)
    y = jnp.dot(o, wo_ref[...], preferred_element_type=F32)
    o_ref[...] = _layer_norm(alpha * x + y, g_ref[...], b_ref[...])


def _cross_attention(x, kv, w_cq, w_co, ln_g, ln_b, alpha, nheads, batch, seq, tm=512):
    t = x.shape[0]
    mlen = kv.shape[0] // batch
    per = seq // tm
    return pl.pallas_call(
        functools.partial(_cross_kernel, alpha, nheads),
        grid=(t // tm,),
        in_specs=[pl.BlockSpec((tm, D_MODEL), lambda i: (i, 0)),
                  pl.BlockSpec((mlen, 2 * D_MODEL), lambda i: (i // per, 0)),
                  _resident((D_MODEL, D_MODEL)), _resident((D_MODEL, D_MODEL)),
                  _resident((1, D_MODEL)), _resident((1, D_MODEL))],
        out_specs=pl.BlockSpec((tm, D_MODEL), lambda i: (i, 0)),
        out_shape=jax.ShapeDtypeStruct((t, D_MODEL), F32),
        compiler_params=_cparams("arbitrary"),
        name="cross_ln2",
    )(x, kv, w_cq, w_co, ln_g.reshape(1, -1), ln_b.reshape(1, -1))


PEER_HEADS = 8
PEER_N_KEYS = 128
PEER_TOPK = 16
PEER_HALF = 128


def _topk_rows(s, nrows):
    rid = lax.broadcasted_iota(jnp.int32, s.shape, 0).astype(F32)
    vals, idxs = [], []
    for _ in range(PEER_TOPK):
        mx = jnp.max(s, axis=0, keepdims=True)
        idx = jnp.min(jnp.where(s == mx, rid, float(nrows)), axis=0, keepdims=True)
        s = jnp.where(rid == idx, -jnp.inf, s)
        vals.append(mx)
        idxs.append(idx)
    return jnp.concatenate(vals, axis=0), jnp.concatenate(idxs, axis=0)


def _pick_rows(sel, table):
    out = jnp.zeros_like(sel)
    for a in range(PEER_TOPK):
        out = jnp.where(sel == float(a), table[a:a + 1, :], out)
    return out


def _peer_route_kernel(x_ref, wq_ref, sk_ref, i_ref, j_ref, g_ref, st_ref, tv_ref, ti_ref, it_ref, jt_ref, gt_ref):
    tm = x_ref.shape[0]
    nhh = 2 * PEER_HEADS
    q = jnp.dot(x_ref[...].astype(BF16), wq_ref[...], preferred_element_type=F32).astype(BF16)
    for hh in range(nhh):
        st_ref[hh] = lax.dot_general(sk_ref[hh], q[:, hh * PEER_HALF:(hh + 1) * PEER_HALF], _NT,
                                     preferred_element_type=F32)
    ngrp = tm // LANES

    def stage1(n, carry):
        h = n // ngrp
        col = pl.multiple_of((n % ngrp) * LANES, LANES)
        for hh in (2 * h, 2 * h + 1):
            vals, idxs = _topk_rows(st_ref[hh, :, pl.ds(col, LANES)], PEER_N_KEYS)
            tv_ref[hh, :, pl.ds(col, LANES)] = vals
            ti_ref[hh, :, pl.ds(col, LANES)] = idxs
        return carry

    lax.fori_loop(0, PEER_HEADS * ngrp, stage1, 0)

    n_full, n_lo = 4, 4
    n_cand = n_full * PEER_TOPK + (PEER_TOPK - n_full) * n_lo

    def head_select(h, col):
        v0 = tv_ref[2 * h, :, pl.ds(col, LANES)]
        v1 = tv_ref[2 * h + 1, :, pl.ds(col, LANES)]
        cand = jnp.concatenate([v0[a:a + 1, :] + v1 for a in range(n_full)]
                               + [v0[a:a + 1, :] + v1[0:n_lo, :] for a in range(n_full, PEER_TOPK)], axis=0)
        best, row = _topk_rows(cand, n_cand)
        tail = row - float(n_full * PEER_TOPK)
        a_head = jnp.floor(row * (1.0 / PEER_TOPK))
        a_tail = jnp.floor(tail * (1.0 / n_lo))
        in_head = row < float(n_full * PEER_TOPK)
        a_sel = jnp.where(in_head, a_head, a_tail + float(n_full))
        b_sel = jnp.where(in_head, row - a_head * PEER_TOPK, tail - a_tail * n_lo)
        ii = _pick_rows(a_sel, ti_ref[2 * h, :, pl.ds(col, LANES)])
        jj = _pick_rows(b_sel, ti_ref[2 * h + 1, :, pl.ds(col, LANES)])
        e = jnp.exp(best - best[0:1, :])
        gate = e / jnp.sum(e, axis=0, keepdims=True)
        r0 = pl.multiple_of(h * PEER_TOPK, PEER_TOPK)
        it_ref[pl.ds(r0, PEER_TOPK), pl.ds(col, LANES)] = ii
        jt_ref[pl.ds(r0, PEER_TOPK), pl.ds(col, LANES)] = jj
        gt_ref[pl.ds(r0, PEER_TOPK), pl.ds(col, LANES)] = gate

    def stage2(n, carry):
        hp = n // ngrp
        col = pl.multiple_of((n % ngrp) * LANES, LANES)
        head_select(2 * hp, col)
        head_select(2 * hp + 1, col)
        return carry

    lax.fori_loop(0, (PEER_HEADS // 2) * ngrp, stage2, 0)
    i_ref[...] = it_ref[...].T.astype(jnp.int32)
    j_ref[...] = jt_ref[...].T.astype(jnp.int32)
    g_ref[...] = gt_ref[...].T


def _peer_route(x, w_pq, sub_keys, tm=256):
    t = x.shape[0]
    nhh = 2 * PEER_HEADS
    out = pl.BlockSpec((tm, LANES), lambda i: (i, 0))
    return pl.pallas_call(
        _peer_route_kernel,
        grid=(t // tm,),
        in_specs=[pl.BlockSpec((tm, D_MODEL), lambda i: (i, 0)),
                  _resident((D_MODEL, nhh * PEER_HALF)),
                  _resident((nhh, PEER_N_KEYS, PEER_HALF))],
        out_specs=[out, out, out],
        out_shape=[jax.ShapeDtypeStruct((t, LANES), jnp.int32), jax.ShapeDtypeStruct((t, LANES), jnp.int32),
                   jax.ShapeDtypeStruct((t, LANES), F32)],
        scratch_shapes=[pltpu.VMEM((nhh, PEER_N_KEYS, tm), F32), pltpu.VMEM((nhh, PEER_TOPK, tm), F32),
                        pltpu.VMEM((nhh, PEER_TOPK, tm), F32), pltpu.VMEM((LANES, tm), F32),
                        pltpu.VMEM((LANES, tm), F32), pltpu.VMEM((LANES, tm), F32)],
        compiler_params=_cparams("arbitrary"),
        name="peer_route",
    )(x, w_pq, sub_keys)


WG_PITCH = 136


def _peer_expert_kernel(alpha, x_ref, i_ref, j_ref, g_ref, u_ref, v_ref, lg_ref, lb_ref, o_ref, wg_ref, acc_ref, xb_ref):
    tm = x_ref.shape[0]
    ec = u_ref.shape[0]
    jc = pl.program_id(1)

    @pl.when(jc == 0)
    def _():
        xb_ref[...] = x_ref[...].astype(BF16)
        acc_ref[...] = jnp.zeros_like(acc_ref)
        rid = lax.broadcasted_iota(jnp.int32, (PEER_N_KEYS, LANES), 0)

        def token(t, carry):
            irow = i_ref[pl.ds(t, 1), :]
            jrow = j_ref[pl.ds(t, 1), :]
            g = g_ref[pl.ds(t, 1), :]
            g_hi = g.astype(BF16).astype(F32)
            g_lo = g - g_hi
            a1 = jnp.where(rid == irow, 1.0, 0.0).astype(BF16)
            hit = rid == jrow
            lhs = jnp.concatenate([a1, a1], axis=1)
            rhs = jnp.concatenate([jnp.where(hit, g_hi, 0.0).astype(BF16), jnp.where(hit, g_lo, 0.0).astype(BF16)], axis=1)
            w = lax.dot_general(lhs, rhs, _NT, preferred_element_type=F32)
            wg_ref[pl.ds(pl.multiple_of(t * WG_PITCH, 8), PEER_N_KEYS), :] = w
            return carry

        lax.fori_loop(0, tm, token, 0, unroll=8)

    hid =lax.dot_general(xb_ref[...], u_ref[...], _NT, preferred_element_type=F32)
    act = 0.5 * hid * (1.0 + lax.erf(hid * (2.0 ** -0.5)))
    nblk = ec // LANES
    parts = []
    for cc in range(nblk):
        wrow = wg_ref[pl.ds(jc * nblk + cc, tm, stride=WG_PITCH), :]
        parts.append((wrow * act[:, cc * LANES:(cc + 1) * LANES]).astype(BF16))
    acc_ref[...] += jnp.dot(jnp.concatenate(parts, axis=1), v_ref[...], preferred_element_type=F32)

    @pl.when(jc == pl.num_programs(1) - 1)
    def _():
        o_ref[...] = _layer_norm(alpha * x_ref[...] + acc_ref[...], lg_ref[...], lb_ref[...])


def _peer_expert(x, ii, jj, gg, u, v, ln_g, ln_b, alpha, tm=512, ec=512):
    t = x.shape[0]
    ne = u.shape[0]
    tok = lambda w: pl.BlockSpec((tm, w), lambda i, j: (i, 0), pipeline_mode=pl.Buffered(1))
    return pl.pallas_call(
        functools.partial(_peer_expert_kernel, alpha),
        grid=(t // tm, ne // ec),
        in_specs=[tok(D_MODEL), tok(LANES), tok(LANES), tok(LANES),
                  pl.BlockSpec((ec, D_MODEL), lambda i, j: (j, 0)), pl.BlockSpec((ec, D_MODEL), lambda i, j: (j, 0)),
                  _resident((1, D_MODEL)), _resident((1, D_MODEL))],
        out_specs=tok(D_MODEL),
        out_shape=jax.ShapeDtypeStruct((t, D_MODEL), F32),
        scratch_shapes=[pltpu.VMEM((tm * WG_PITCH, LANES), F32), pltpu.VMEM((tm, D_MODEL), F32),
                        pltpu.VMEM((tm, D_MODEL), BF16)],
        compiler_params=_cparams("arbitrary", "arbitrary"),
        name="peer_expert_ln3",
    )(x, ii, jj, gg, u, v, ln_g.reshape(1, -1), ln_b.reshape(1, -1))


def kernel(x, mem, w_in, conv_w, conv_b, dt_bias, a_log, d_skip, ssd_norm_w, w_ssd_br, lam_q, lam_k, subln_w, w_diff_br, gate_bias, w_o, ln1_g, ln1_b, w_cq, w_ck, w_cv, w_co, ln2_g, ln2_b, w_pq, sub_keys, peer_u, peer_v, ln3_g, ln3_b):
    batch, seq, d_model = x.shape
    depth = w_in.shape[0]
    t = batch * seq
    assert d_model == D_MODEL and w_in.shape[2] == D_INNER + D_XBC + SSD_HEADS + 5 * D_MODEL
    assert seq % 512 == 0 and sub_keys.shape[1:] == (PEER_HEADS, 2, PEER_N_KEYS, PEER_HALF)
    alpha = (2 * depth) ** 0.25
    mem_heads = 4
    cuts = [D_INNER, D_INNER + D_XBC, D_INNER + D_XBC + SSD_HEADS]
    xf = x.reshape(t, d_model)
    memf = mem.reshape(batch * mem.shape[1], d_model).astype(BF16)
    for l in range(depth):
        lambda_init = 0.8 - 0.6 * math.exp(-0.3 * l)
        w = w_in[l].astype(BF16)
        w_dt = jnp.pad(w[:, cuts[1]:cuts[2]], ((0, 0), (0, LANES - SSD_HEADS)))
        xb = xf.astype(BF16)
        z = _matmul(xb, w[:, :cuts[0]], F32, 512, 1024)
        xbc = _matmul(xb, w[:, cuts[0]:cuts[1]], F32, 512, 1024)
        dt_raw = _matmul(xb, w_dt, F32, 512, LANES)
        q_scale = jnp.concatenate([jnp.full((D_MODEL,), DIFF_HEAD_DIM ** -0.5 * math.log2(math.e), F32),
                                   jnp.ones((2 * D_MODEL,), F32)])
        qkv = _matmul(xb, w[:, cuts[2]:cuts[2] + 3 * D_MODEL], BF16, 512, 1024, col_scale=q_scale)
        gates = _matmul(xb, w[:, cuts[2] + 3 * D_MODEL:], F32, 512, 1024)
        xbc_act = _conv_silu(xbc, conv_w[l].astype(F32), conv_b[l].astype(F32), seq)
        y_ssd = _ssd(xbc_act, z, dt_raw, dt_bias[l], a_log[l], d_skip[l], ssd_norm_w[l], batch, seq)
        o_att = _diff_attention(qkv, lam_q[l], lam_k[l], subln_w[l], lambda_init, batch, seq)
        xf = _mix(y_ssd, o_att, gates, xf, w_ssd_br[l].astype(BF16), w_diff_br[l].astype(BF16), w_o[l].astype(BF16),
                  gate_bias[l], ln1_g[l], ln1_b[l], alpha)
        kv = _matmul(memf, jnp.concatenate([w_ck[l], w_cv[l]], axis=1).astype(BF16), BF16, memf.shape[0], 1024)
        xf = _cross_attention(xf, kv, w_cq[l].astype(BF16), w_co[l].astype(BF16), ln2_g[l], ln2_b[l], alpha,
                              mem_heads, batch, seq)
        ii, jj, gg = _peer_route(xf, w_pq[l].astype(BF16),
                                 sub_keys[l].reshape(2 * PEER_HEADS, PEER_N_KEYS, PEER_HALF).astype(BF16))
        xf = _peer_expert(xf, ii, jj, gg, peer_u[l].astype(BF16), peer_v[l].astype(BF16), ln3_g[l], ln3_b[l], alpha)
    return xf.reshape(batch, seq, d_model)
```

```python
import functools
import math

import jax
import jax.numpy as jnp
from jax import lax
from jax.experimental import pallas as pl
from jax.experimental.pallas import tpu as pltpu

F32 = jnp.float32
BF16 = jnp.bfloat16

D_MODEL = 1024
D_INNER = 2048
SSD_HEADS = 32
SSD_HEAD_DIM = 64
SSD_GROUPS = 4
D_STATE = 128
CONV_WIDTH = 4
SSD_CHUNK = 128
D_XBC = D_INNER + 2 * SSD_GROUPS * D_STATE
DIFF_HEADS = 8
DIFF_HEAD_DIM = 64
DIFF_V_DIM = 128
RMS_EPS = 1e-5
LN_EPS = 1e-5
LANES = 128
VMEM_LIMIT = 56 * 1024 * 1024

_NT = (((1,), (1,)), ((), ()))
_TN = (((0,), (0,)), ((), ()))


def _cparams(*sem):
    return pltpu.CompilerParams(dimension_semantics=sem, vmem_limit_bytes=VMEM_LIMIT)


def _mm_kernel(a_ref, w_ref, o_ref):
    o_ref[...] = jnp.dot(a_ref[...], w_ref[...], preferred_element_type=F32).astype(o_ref.dtype)


def _mm_scaled_kernel(a_ref, w_ref, c_ref, o_ref):
    o_ref[...] = (jnp.dot(a_ref[...], w_ref[...], preferred_element_type=F32) * c_ref[...]).astype(o_ref.dtype)


def _matmul(a, w, out_dtype, tm, tn, col_scale=None):
    m, k = a.shape
    n = w.shape[1]
    assert m % tm == 0 and n % tn == 0
    in_specs = [pl.BlockSpec((tm, k), lambda j, i: (i, 0)), pl.BlockSpec((k, tn), lambda j, i: (0, j))]
    operands = [a, w]
    if col_scale is not None:
        in_specs.append(pl.BlockSpec((1, tn), lambda j, i: (0, j)))
        operands.append(col_scale.reshape(1, n))
    return pl.pallas_call(
        _mm_kernel if col_scale is None else _mm_scaled_kernel,
        grid=(n // tn, m // tm),
        in_specs=in_specs,
        out_specs=pl.BlockSpec((tm, tn), lambda j, i: (i, j)),
        out_shape=jax.ShapeDtypeStruct((m, n), out_dtype),
        compiler_params=_cparams("arbitrary", "arbitrary"),
        name="matmul",
    )(*operands)


def _silu(x):
    return x * (1.0 / (1.0 + jnp.exp(-x)))


def _conv_kernel(tiles_per_seq, x_ref, halo_ref, w_ref, b_ref, o_ref):
    ts = x_ref.shape[0]
    first = (pl.program_id(0) % tiles_per_seq) == 0
    halo = jnp.where(first, 0.0, halo_ref[...])
    xc = jnp.concatenate([halo, x_ref[...]], axis=0)
    acc = b_ref[...] + w_ref[CONV_WIDTH - 1:CONV_WIDTH, :] * x_ref[...]
    for kk in range(CONV_WIDTH - 1):
        off = 8 - (CONV_WIDTH - 1) + kk
        acc = acc + w_ref[kk:kk + 1, :] * xc[off:off + ts, :]
    o_ref[...] = _silu(acc)


def _conv_silu(x, w, b, seq, ts=512, tc=1024):
    t, c = x.shape
    assert seq % ts == 0 and c % tc == 0 and ts % 8 == 0
    hb = ts // 8
    return pl.pallas_call(
        functools.partial(_conv_kernel, seq // ts),
        grid=(t // ts, c // tc),
        in_specs=[
            pl.BlockSpec((ts, tc), lambda i, j: (i, j)),
            pl.BlockSpec((8, tc), lambda i, j: (jnp.maximum(i * hb - 1, 0), j)),
            pl.BlockSpec((CONV_WIDTH, tc), lambda i, j: (0, j)),
            pl.BlockSpec((1, tc), lambda i, j: (0, j)),
        ],
        out_specs=pl.BlockSpec((ts, tc), lambda i, j: (i, j)),
        out_shape=jax.ShapeDtypeStruct((t, c), F32),
        compiler_params=_cparams("arbitrary", "arbitrary"),
        name="conv_silu",
    )(x, x, w, b.reshape(1, c))


def _split3(v):
    hi = v.astype(BF16)
    r1 = v - hi.astype(F32)
    mid = r1.astype(BF16)
    lo = (r1 - mid.astype(F32)).astype(BF16)
    return jnp.concatenate([hi, mid, lo], axis=1)


def _ssd_kernel(xbc_ref, z_ref, dt_ref, dtb_ref, alog_ref, dskip_ref, e3_ref, nw_ref, o_ref, state_ref):
    L = SSD_CHUNK
    gw = D_INNER // SSD_GROUPS

    @pl.when(pl.program_id(1) == 0)
    def _():
        state_ref[...] = jnp.zeros_like(state_ref)

    x_dt = dt_ref[...] + dtb_ref[...]
    dt = jnp.maximum(x_dt, 0.0) + jnp.log1p(jnp.exp(-jnp.abs(x_dt)))
    a_neg = -jnp.exp(alog_ref[...])
    A = dt * a_neg
    row = lax.broadcasted_iota(jnp.int32, (L, L), 0)
    col = lax.broadcasted_iota(jnp.int32, (L, L), 1)
    tril = col <= row
    acs = jnp.dot(tril.astype(F32), A, precision=lax.Precision.HIGHEST, preferred_element_type=F32)
    a_last = acs[L - 1:L, :]
    w_state = dt * jnp.exp(a_last - acs)
    exp_a = jnp.exp(acs)
    cd = jnp.exp(jnp.broadcast_to(a_last, (8, LANES)))
    e3 = e3_ref[...]
    w_x = jnp.dot(_split3(w_state), e3, preferred_element_type=F32)
    expa_x = jnp.dot(_split3(exp_a), e3, preferred_element_type=F32)
    cd_x = jnp.dot(_split3(cd), e3, preferred_element_type=F32)[0:1, :]
    acs_t = acs.T
    dt_t = dt.T
    lane = lax.broadcasted_iota(jnp.int32, (L, LANES), 1)
    lo_half = lane < SSD_HEAD_DIM

    for g in range(SSD_GROUPS):
        bg = xbc_ref[:, D_INNER + g * D_STATE:D_INNER + (g + 1) * D_STATE].astype(BF16)
        cg = xbc_ref[:, D_INNER + SSD_GROUPS * D_STATE + g * D_STATE:D_INNER + SSD_GROUPS * D_STATE + (g + 1) * D_STATE].astype(BF16)
        xg = xbc_ref[:, g * gw:(g + 1) * gw]
        cb = lax.dot_general(cg, bg, _NT, preferred_element_type=F32)
        sg = state_ref[g]
        y_off = jnp.dot(cg, sg.astype(BF16), preferred_element_type=F32) * expa_x[:, g * gw:(g + 1) * gw]
        xw = (xg * w_x[:, g * gw:(g + 1) * gw]).astype(BF16)
        new_state = lax.dot_general(bg, xw, _TN, preferred_element_type=F32)
        state_ref[g] = sg * cd_x[:, g * gw:(g + 1) * gw] + new_state
        y_pairs = []
        for j in range(gw // LANES):
            ms = []
            for hh in range(2):
                h = g * (gw // SSD_HEAD_DIM) + 2 * j + hh
                seg = acs[:, h:h + 1] - acs_t[h:h + 1, :]
                ms.append(cb * jnp.exp(jnp.where(tril, seg, -jnp.inf)) * dt_t[h:h + 1, :])
            lhs = jnp.concatenate(ms, axis=1).astype(BF16)
            xp = xg[:, j * LANES:(j + 1) * LANES]
            rhs = jnp.concatenate([jnp.where(lo_half, xp, 0.0), jnp.where(lo_half, 0.0, xp)], axis=0).astype(BF16)
            y_pairs.append(jnp.dot(lhs, rhs, preferred_element_type=F32))
        y = jnp.concatenate(y_pairs, axis=1) + y_off + dskip_ref[:, g * gw:(g + 1) * gw] * xg
        y = y * _silu(z_ref[:, g * gw:(g + 1) * gw])
        ms_ = jnp.mean(y * y, axis=1, keepdims=True)
        y = y * lax.rsqrt(ms_ + RMS_EPS) * nw_ref[:, g * gw:(g + 1) * gw]
        o_ref[:, g * gw:(g + 1) * gw] = y.astype(o_ref.dtype)


def _head_expand_matrix():
    head_of_lane = jnp.arange(D_INNER) // SSD_HEAD_DIM
    e = (jnp.arange(LANES)[:, None] == head_of_lane[None, :]).astype(BF16)
    return jnp.concatenate([e, e, e], axis=0)


def _pad_lanes(v, fill=0.0):
    return jnp.pad(v.astype(F32), (0, LANES - v.shape[0]), constant_values=fill).reshape(1, LANES)


def _ssd(xbc_act, z, dt_raw, dt_bias, a_log, d_skip, norm_w, batch, seq):
    t = xbc_act.shape[0]
    nchunk = seq // SSD_CHUNK
    row = lambda b, c: (b * nchunk + c, 0)
    const = lambda b, c: (0, 0)
    dskip_x = jnp.repeat(d_skip.astype(F32), SSD_HEAD_DIM).reshape(1, D_INNER)
    return pl.pallas_call(
        _ssd_kernel,
        grid=(batch, nchunk),
        in_specs=[
            pl.BlockSpec((SSD_CHUNK, D_XBC), row),
            pl.BlockSpec((SSD_CHUNK, D_INNER), row),
            pl.BlockSpec((SSD_CHUNK, LANES), row),
            pl.BlockSpec((1, LANES), const),
            pl.BlockSpec((1, LANES), const),
            pl.BlockSpec((1, D_INNER), const),
            pl.BlockSpec((3 * LANES, D_INNER), const),
            pl.BlockSpec((1, D_INNER), const),
        ],
        out_specs=pl.BlockSpec((SSD_CHUNK, D_INNER), row),
        out_shape=jax.ShapeDtypeStruct((t, D_INNER), BF16),
        scratch_shapes=[pltpu.VMEM((SSD_GROUPS, D_STATE, D_INNER // SSD_GROUPS), F32)],
        compiler_params=_cparams("arbitrary", "arbitrary"),
        name="ssd",
    )(xbc_act, z, dt_raw, _pad_lanes(dt_bias), _pad_lanes(a_log), dskip_x, _head_expand_matrix(),
      norm_w.astype(F32).reshape(1, D_INNER))


def _attn_kernel(lambda_init, q_ref, k_ref, v_ref, lq_ref, lk_ref, sw_ref, o_ref, q2_ref, s_ref, smax_ref, m_ref, l_ref,
                 acc_ref):
    tq = q_ref.shape[0]
    qi = pl.program_id(2)
    lane = lax.broadcasted_iota(jnp.int32, (tq, LANES), 1)
    q = q_ref[...]
    zero = jnp.zeros_like(q)
    q2_ref[0:tq, :] = jnp.where(lane < DIFF_HEAD_DIM, q, zero)
    q2_ref[tq:2 * tq, :] = jnp.where(lane < DIFF_HEAD_DIM, zero, q)
    m_ref[...] = jnp.full_like(m_ref, -jnp.inf)
    l_ref[...] = jnp.zeros_like(l_ref)
    acc_ref[...] = jnp.zeros_like(acc_ref)

    def produce(kj, slot, keep_all):
        k = k_ref[pl.ds(pl.multiple_of(kj * tq, tq), tq), :]
        s_blk = lax.dot_general(k, q2_ref[...], _NT, preferred_element_type=F32)
        if keep_all is not True:
            kpos = lax.broadcasted_iota(jnp.int32, s_blk.shape, 0)
            qpos = lax.broadcasted_iota(jnp.int32, s_blk.shape, 1) & (tq - 1)
            s_blk = jnp.where((kpos <= qpos) | keep_all, s_blk, -jnp.inf)
        s_ref[slot] = s_blk
        smax_ref[slot] = jnp.max(s_blk, axis=0, keepdims=True)

    def consume(kj, slot):
        v = v_ref[pl.ds(pl.multiple_of(kj * tq, tq), tq), :]
        m_old = m_ref[...]
        m_new = jnp.maximum(m_old, smax_ref[slot])
        alpha = jnp.exp2(m_old - m_new)
        p = jnp.exp2(s_ref[slot] - m_new)
        l_ref[...] = alpha * l_ref[...] + jnp.sum(p, axis=0, keepdims=True)
        acc_ref[...] = acc_ref[...] * alpha + lax.dot_general(v, p.astype(BF16), _TN, preferred_element_type=F32)
        m_ref[...] = m_new

    def by_parity(idx, fn):
        lax.cond((idx & 1) == 0, lambda: fn(0), lambda: fn(1))

    produce(0, 0, qi > 0)

    def body(kj, carry):
        def step(cur):
            produce(kj + 1, 1 - cur, True)
            consume(kj, cur)
        by_parity(kj, step)
        return carry

    lax.fori_loop(0, qi - 1, body, 0)

    @pl.when(qi >= 1)
    def _():
        def step(cur):
            produce(qi, 1 - cur, False)
            consume(qi - 1, cur)
        by_parity(qi - 1, step)

    by_parity(qi, lambda cur: consume(qi, cur))

    o_all = acc_ref[...] / l_ref[...]
    lam = (jnp.exp(jnp.sum(lq_ref[0:1, :] * lk_ref[0:1, :], axis=1, keepdims=True))
           - jnp.exp(jnp.sum(lq_ref[1:2, :] * lk_ref[1:2, :], axis=1, keepdims=True)) + lambda_init)
    o = o_all[:, :tq] - lam * o_all[:, tq:]
    ms = jnp.mean(o * o, axis=0, keepdims=True)
    o = o * lax.rsqrt(ms + RMS_EPS) * sw_ref[...] * (1.0 - lambda_init)
    o_ref[...] = o.T.astype(o_ref.dtype)


def _diff_attention(qkv, lam_q, lam_k, subln_w, lambda_init, batch, seq, tq=512):
    t = qkv.shape[0]
    nq = seq // tq
    nh = DIFF_HEADS
    assert tq & (tq - 1) == 0 and seq % tq == 0
    return pl.pallas_call(
        functools.partial(_attn_kernel, lambda_init),
        grid=(batch, nh, nq),
        in_specs=[
            pl.BlockSpec((tq, LANES), lambda b, h, i: (b * nq + i, h)),
            pl.BlockSpec((seq, LANES), lambda b, h, i: (b, nh + h)),
            pl.BlockSpec((seq, LANES), lambda b, h, i: (b, 2 * nh + h)),
            pl.BlockSpec((2, DIFF_HEAD_DIM), lambda b, h, i: (0, 0)),
            pl.BlockSpec((2, DIFF_HEAD_DIM), lambda b, h, i: (0, 0)),
            pl.BlockSpec((DIFF_V_DIM, 1), lambda b, h, i: (0, 0)),
        ],
        out_specs=pl.BlockSpec((tq, LANES), lambda b, h, i: (b * nq + i, h)),
        out_shape=jax.ShapeDtypeStruct((t, nh * DIFF_V_DIM), BF16),
        scratch_shapes=[pltpu.VMEM((2 * tq, LANES), BF16), pltpu.VMEM((2, tq, 2 * tq), F32),
                        pltpu.VMEM((2, 1, 2 * tq), F32), pltpu.VMEM((1, 2 * tq), F32), pltpu.VMEM((1, 2 * tq), F32),
                        pltpu.VMEM((DIFF_V_DIM, 2 * tq), F32)],
        compiler_params=_cparams("arbitrary", "arbitrary", "arbitrary"),
        name="diff_attention",
    )(qkv, qkv, qkv, lam_q.astype(F32), lam_k.astype(F32), subln_w.astype(F32).reshape(DIFF_V_DIM, 1))


def _layer_norm(v, g, b):
    mu = jnp.mean(v, axis=-1, keepdims=True)
    d = v - mu
    var = jnp.mean(d * d, axis=-1, keepdims=True)
    return d * lax.rsqrt(var + LN_EPS) * g + b


def _sigmoid(v):
    return 1.0 / (1.0 + jnp.exp(-v))


def _mix_kernel(alpha, ys_ref, oa_ref, gates_ref, x_ref, wssd_ref, wdiff_ref, wo_ref, gb_ref, g_ref, b_ref, o_ref):
    y_ssd = jnp.dot(ys_ref[...], wssd_ref[...], preferred_element_type=F32)
    y_att = jnp.dot(oa_ref[...], wdiff_ref[...], preferred_element_type=F32)
    ga = _sigmoid(gates_ref[:, :D_MODEL] + gb_ref[0:1, :])
    gb = _sigmoid(gates_ref[:, D_MODEL:] + gb_ref[1:2, :])
    mix = jnp.dot((ga * y_ssd + gb * y_att).astype(BF16), wo_ref[...], preferred_element_type=F32)
    o_ref[...] = _layer_norm(alpha * x_ref[...] + mix, g_ref[...], b_ref[...])


def _resident(shape):
    return pl.BlockSpec(shape, lambda *_: (0,) * len(shape))


def _mix(ys, oa, gates, x, w_ssd, w_diff, w_o, gate_bias, ln_g, ln_b, alpha, tm=512):
    t = x.shape[0]
    rows = lambda w: pl.BlockSpec((tm, w), lambda i: (i, 0))
    return pl.pallas_call(
        functools.partial(_mix_kernel, alpha),
        grid=(t // tm,),
        in_specs=[rows(D_INNER), rows(D_MODEL), rows(2 * D_MODEL), rows(D_MODEL),
                  _resident((D_INNER, D_MODEL)), _resident((D_MODEL, D_MODEL)), _resident((D_MODEL, D_MODEL)),
                  _resident((2, D_MODEL)), _resident((1, D_MODEL)), _resident((1, D_MODEL))],
        out_specs=rows(D_MODEL),
        out_shape=jax.ShapeDtypeStruct((t, D_MODEL), F32),
        compiler_params=_cparams("arbitrary"),
        name="mix_ln1",
    )(ys, oa, gates, x, w_ssd, w_diff, w_o, gate_bias.astype(F32), ln_g.reshape(1, -1), ln_b.reshape(1, -1))


def _cross_kernel(alpha, nheads, x_ref, kv_ref, wq_ref, wo_ref, g_ref, b_ref, o_ref):
    x = x_ref[...]
    hd = D_MODEL // nheads
    q = jnp.dot(x.astype(BF16), wq_ref[...], preferred_element_type=F32).astype(BF16)
    outs = []
    for h in range(nheads):
        k = kv_ref[:, h * hd:(h + 1) * hd]
        v = kv_ref[:, D_MODEL + h * hd:D_MODEL + (h + 1) * hd]
        s = lax.dot_general(q[:, h * hd:(h + 1) * hd], k, _NT, preferred_element_type=F32) * (hd ** -0.5)
        e = jnp.exp(s - jnp.max(s, axis=-1, keepdims=True))
        p = e / jnp.sum(e, axis=-1, keepdims=True)
        outs.append(jnp.dot(p.astype(BF16), v, preferred_element_type=F32))
    o = jnp.concatenate(outs, axis=1).astype(BF16)
    y = jnp.dot(o, wo_ref[...], preferred_element_type=F32)
    o_ref[...] = _layer_norm(alpha * x + y, g_ref[...], b_ref[...])


def _cross_attention(x, kv, w_cq, w_co, ln_g, ln_b, alpha, nheads, batch, seq, tm=512):
    t = x.shape[0]
    mlen = kv.shape[0] // batch
    per = seq // tm
    return pl.pallas_call(
        functools.partial(_cross_kernel, alpha, nheads),
        grid=(t // tm,),
        in_specs=[pl.BlockSpec((tm, D_MODEL), lambda i: (i, 0)),
                  pl.BlockSpec((mlen, 2 * D_MODEL), lambda i: (i // per, 0)),
                  _resident((D_MODEL, D_MODEL)), _resident((D_MODEL, D_MODEL)),
                  _resident((1, D_MODEL)), _resident((1, D_MODEL))],
        out_specs=pl.BlockSpec((tm, D_MODEL), lambda i: (i, 0)),
        out_shape=jax.ShapeDtypeStruct((t, D_MODEL), F32),
        compiler_params=_cparams("arbitrary"),
        name="cross_ln2",
    )(x, kv, w_cq, w_co, ln_g.reshape(1, -1), ln_b.reshape(1, -1))


PEER_HEADS = 8
PEER_N_KEYS = 128
PEER_TOPK = 16
PEER_HALF = 128
WG_PITCH = 136
STAIR_FULL = 4
STAIR_LO = 4
STAIR_ROWS = STAIR_FULL * PEER_TOPK + (PEER_TOPK - STAIR_FULL) * STAIR_LO


def _topk_rows(s, nrows):
    rid = lax.broadcasted_iota(jnp.int32, s.shape, 0).astype(F32)
    vals, idxs = [], []
    for _ in range(PEER_TOPK):
        mx = jnp.max(s, axis=0, keepdims=True)
        idx = jnp.min(jnp.where(s == mx, rid, float(nrows)), axis=0, keepdims=True)
        s = jnp.where(rid == idx, -jnp.inf, s)
        vals.append(mx)
        idxs.append(idx)
    return jnp.concatenate(vals, axis=0), jnp.concatenate(idxs, axis=0)


def _pick_rows(sel, table):
    out = jnp.zeros_like(sel)
    for a in range(PEER_TOPK):
        out = jnp.where(sel == float(a), table[a:a + 1, :], out)
    return out


def _select_experts(v0, i0, v1, i1):
    cand = jnp.concatenate([v0[a:a + 1, :] + v1 for a in range(STAIR_FULL)]
                           + [v0[a:a + 1, :] + v1[0:STAIR_LO, :] for a in range(STAIR_FULL, PEER_TOPK)], axis=0)
    best, row = _topk_rows(cand, STAIR_ROWS)
    tail = row - float(STAIR_FULL * PEER_TOPK)
    a_head = jnp.floor(row * (1.0 / PEER_TOPK))
    a_tail = jnp.floor(tail * (1.0 / STAIR_LO))
    in_head = row < float(STAIR_FULL * PEER_TOPK)
    a_sel = jnp.where(in_head, a_head, a_tail + float(STAIR_FULL))
    b_sel = jnp.where(in_head, row - a_head * PEER_TOPK, tail - a_tail * STAIR_LO)
    e = jnp.exp(best - best[0:1, :])
    return _pick_rows(a_sel, i0), _pick_rows(b_sel, i1), e / jnp.sum(e, axis=0, keepdims=True)


def _peer_kernel(alpha, ntiles, xr_ref, xd_ref, wq_ref, sk_ref, u_ref, v_ref, lg_ref, lb_ref, o_ref,
                 st_ref, tv_ref, ti_ref, it_ref, jt_ref, gt_ref, ir_ref, jr_ref, gr_ref, wg_ref, xb_ref):
    i = pl.program_id(0)
    jc = pl.program_id(1)
    tm = xr_ref.shape[0]
    ec = u_ref.shape[0]
    ngrp = tm // LANES
    nhh = 2 * PEER_HEADS
    has_dense = i >= 1
    has_route = i < ntiles

    @pl.when((jc == 0) & has_dense)
    def _():
        for g in range(ngrp):
            ir_ref[g * LANES:(g + 1) * LANES, :] = it_ref[g].T.astype(jnp.int32)
            jr_ref[g * LANES:(g + 1) * LANES, :] = jt_ref[g].T.astype(jnp.int32)
            gr_ref[g * LANES:(g + 1) * LANES, :] = gt_ref[g].T
        xb_ref[...] = xd_ref[...].astype(BF16)
        o_ref[...] = jnp.zeros_like(o_ref)
        rid = lax.broadcasted_iota(jnp.int32, (PEER_N_KEYS, LANES), 0)

        def token(t, carry):
            irow = ir_ref[pl.ds(t, 1), :]
            jrow = jr_ref[pl.ds(t, 1), :]
            g = gr_ref[pl.ds(t, 1), :]
            g_hi = g.astype(BF16).astype(F32)
            g_lo = g - g_hi
            a1 = jnp.where(rid == irow, 1.0, 0.0).astype(BF16)
            hit = rid == jrow
            lhs = jnp.concatenate([a1, a1], axis=1)
            rhs = jnp.concatenate([jnp.where(hit, g_hi, 0.0).astype(BF16), jnp.where(hit, g_lo, 0.0).astype(BF16)], axis=1)
            w = lax.dot_general(lhs, rhs, _NT, preferred_element_type=F32)
            wg_ref[pl.ds(pl.multiple_of(t * WG_PITCH, 8), PEER_N_KEYS), :] = w
            return carry

        lax.fori_loop(0, tm, token, 0, unroll=32)

    @pl.when((jc == 0) & (i == 0))
    def _():
        tv_ref[...] = jnp.zeros_like(tv_ref)
        ti_ref[...] = jnp.zeros_like(ti_ref)

    @pl.when((jc == 0) & has_route)
    def _():
        q = jnp.dot(xr_ref[...].astype(BF16), wq_ref[...], preferred_element_type=F32).astype(BF16)
        for hh in range(nhh):
            st_ref[hh] = lax.dot_general(sk_ref[hh], q[:, hh * PEER_HALF:(hh + 1) * PEER_HALF], _NT,
                                         preferred_element_type=F32)

    def route_scores(item):
        h = item % PEER_HEADS
        tok0 = pl.multiple_of((item // PEER_HEADS) * LANES, LANES)
        for half in range(2):
            vals, idxs = _topk_rows(st_ref[2 * h + half, :, pl.ds(tok0, LANES)], PEER_N_KEYS)
            tv_ref[item & 1, half] = vals
            ti_ref[item & 1, half] = idxs

    def route_select(item):
        slot = item & 1
        ii, jj, gate = _select_experts(tv_ref[slot, 0], ti_ref[slot, 0], tv_ref[slot, 1], ti_ref[slot, 1])
        g = item // PEER_HEADS
        r0 = pl.multiple_of((item % PEER_HEADS) * PEER_TOPK, PEER_TOPK)
        it_ref[g, pl.ds(r0, PEER_TOPK), :] = ii
        jt_ref[g, pl.ds(r0, PEER_TOPK), :] = jj
        gt_ref[g, pl.ds(r0, PEER_TOPK), :] = gate

    def route_item():
        route_select(jnp.maximum(jc - 1, 0))
        route_scores(jc)

    def dense_step():
        hid = lax.dot_general(xb_ref[...], u_ref[...], _NT, preferred_element_type=F32)
        act = 0.5 * hid * (1.0 + lax.erf(hid * (2.0 ** -0.5)))
        nblk = ec // LANES
        parts = []
        for cc in range(nblk):
            wrow = wg_ref[pl.ds(jc * nblk + cc, tm, stride=WG_PITCH), :]
            parts.append((wrow * act[:, cc * LANES:(cc + 1) * LANES]).astype(BF16))
        o_ref[...] += jnp.dot(jnp.concatenate(parts, axis=1), v_ref[...], preferred_element_type=F32)

    @pl.when(has_dense & has_route)
    def _():
        route_item()
        dense_step()

    @pl.when(jnp.logical_not(has_dense))
    def _():
        route_item()

    @pl.when(jnp.logical_not(has_route))
    def _():
        dense_step()

    last = pl.num_programs(1) - 1

    @pl.when((jc == last) & has_route)
    def _():
        route_select(last)

    @pl.when((jc == last) & has_dense)
    def _():
        o_ref[...] = _layer_norm(alpha * xd_ref[...] + o_ref[...], lg_ref[...], lb_ref[...])


def _peer(x, w_pq, sub_keys, u, v, ln_g, ln_b, alpha, tm=256):
    t = x.shape[0]
    ne = u.shape[0]
    ntiles = t // tm
    ngrp = tm // LANES
    nsteps = PEER_HEADS * ngrp
    ec = ne // nsteps
    assert ne == PEER_N_KEYS * PEER_N_KEYS and ne % nsteps == 0 and ec % LANES == 0
    once = dict(pipeline_mode=pl.Buffered(1))
    tile_prev = lambda i, j: (jnp.maximum(i - 1, 0), 0)
    return pl.pallas_call(
        functools.partial(_peer_kernel, alpha, ntiles),
        grid=(ntiles + 1, nsteps),
        in_specs=[pl.BlockSpec((tm, D_MODEL), lambda i, j: (jnp.minimum(i, ntiles - 1), 0), **once),
                  pl.BlockSpec((tm, D_MODEL), tile_prev, **once),
                  pl.BlockSpec((D_MODEL, 2 * PEER_HEADS * PEER_HALF), lambda i, j: (0, 0), **once),
                  pl.BlockSpec((2 * PEER_HEADS, PEER_N_KEYS, PEER_HALF), lambda i, j: (0, 0, 0), **once),
                  pl.BlockSpec((ec, D_MODEL), lambda i, j: (j, 0)), pl.BlockSpec((ec, D_MODEL), lambda i, j: (j, 0)),
                  _resident((1, D_MODEL)), _resident((1, D_MODEL))],
        out_specs=pl.BlockSpec((tm, D_MODEL), tile_prev, **once),
        out_shape=jax.ShapeDtypeStruct((t, D_MODEL), F32),
        scratch_shapes=[pltpu.VMEM((2 * PEER_HEADS, PEER_N_KEYS, tm), F32),
                        pltpu.VMEM((2, 2, PEER_TOPK, LANES), F32), pltpu.VMEM((2, 2, PEER_TOPK, LANES), F32),
                        pltpu.VMEM((ngrp, LANES, LANES), F32), pltpu.VMEM((ngrp, LANES, LANES), F32),
                        pltpu.VMEM((ngrp, LANES, LANES), F32),
                        pltpu.VMEM((tm, LANES), jnp.int32), pltpu.VMEM((tm, LANES), jnp.int32),
                        pltpu.VMEM((tm, LANES), F32),
                        pltpu.VMEM((tm * WG_PITCH, LANES), F32), pltpu.VMEM((tm, D_MODEL), BF16)],
        compiler_params=_cparams("arbitrary", "arbitrary"),
        name="peer_ln3",
    )(x, x, w_pq, sub_keys, u, v, ln_g.reshape(1, -1), ln_b.reshape(1, -1))


def kernel(x, mem, w_in, conv_w, conv_b, dt_bias, a_log, d_skip, ssd_norm_w, w_ssd_br, lam_q, lam_k, subln_w, w_diff_br, gate_bias, w_o, ln1_g, ln1_b, w_cq, w_ck, w_cv, w_co, ln2_g, ln2_b, w_pq, sub_keys, peer_u, peer_v, ln3_g, ln3_b):
    batch, seq, d_model = x.shape
    depth = w_in.shape[0]
    t = batch * seq
    assert d_model == D_MODEL and w_in.shape[2] == D_INNER + D_XBC + SSD_HEADS + 5 * D_MODEL
    assert seq % 512 == 0 and sub_keys.shape[1:] == (PEER_HEADS, 2, PEER_N_KEYS, PEER_HALF)
    alpha = (2 * depth) ** 0.25
    mem_heads = 4
    cuts = [D_INNER, D_INNER + D_XBC, D_INNER + D_XBC + SSD_HEADS]
    xf = x.reshape(t, d_model)
    memf = mem.reshape(batch * mem.shape[1], d_model).astype(BF16)
    for l in range(depth):
        lambda_init = 0.8 - 0.6 * math.exp(-0.3 * l)
        w = w_in[l].astype(BF16)
        w_dt = jnp.pad(w[:, cuts[1]:cuts[2]], ((0, 0), (0, LANES - SSD_HEADS)))
        xb = xf.astype(BF16)
        z = _matmul(xb, w[:, :cuts[0]], F32, 1024, 1024)
        xbc = _matmul(xb, w[:, cuts[0]:cuts[1]], F32, 1024, 1024)
        dt_raw = _matmul(xb, w_dt, F32, 512, LANES)
        q_scale = jnp.concatenate([jnp.full((D_MODEL,), DIFF_HEAD_DIM ** -0.5 * math.log2(math.e), F32),
                                   jnp.ones((2 * D_MODEL,), F32)])
        qkv = _matmul(xb, w[:, cuts[2]:cuts[2] + 3 * D_MODEL], BF16, 1024, 1024, col_scale=q_scale)
        gates = _matmul(xb, w[:, cuts[2] + 3 * D_MODEL:], F32, 1024, 1024)
        xbc_act = _conv_silu(xbc, conv_w[l].astype(F32), conv_b[l].astype(F32), seq)
        y_ssd = _ssd(xbc_act, z, dt_raw, dt_bias[l], a_log[l], d_skip[l], ssd_norm_w[l], batch, seq)
        o_att = _diff_attention(qkv, lam_q[l], lam_k[l], subln_w[l], lambda_init, batch, seq)
        xf = _mix(y_ssd, o_att, gates, xf, w_ssd_br[l].astype(BF16), w_diff_br[l].astype(BF16), w_o[l].astype(BF16),
                  gate_bias[l], ln1_g[l], ln1_b[l], alpha)
        kv = _matmul(memf, jnp.concatenate([w_ck[l], w_cv[l]], axis=1).astype(BF16), BF16, memf.shape[0], 1024)
        xf = _cross_attention(xf, kv, w_cq[l].astype(BF16), w_co[l].astype(BF16), ln2_g[l], ln2_b[l], alpha,
                              mem_heads, batch, seq)
        xf = _peer(xf, w_pq[l].astype(BF16), sub_keys[l].reshape(2 * PEER_HEADS, PEER_N_KEYS, PEER_HALF).astype(BF16),
                   peer_u[l].astype(BF16), peer_v[l].astype(BF16), ln3_g[l], ln3_b[l], alpha)
    return xf.reshape(batch, seq, d_model)
```

```python
import functools
import math

import jax
import jax.numpy as jnp
from jax import lax
from jax.experimental import pallas as pl
from jax.experimental.pallas import tpu as pltpu

F32 = jnp.float32
BF16 = jnp.bfloat16

D_MODEL = 1024
D_INNER = 2048
SSD_HEADS = 32
SSD_HEAD_DIM = 64
SSD_GROUPS = 4
D_STATE = 128
CONV_WIDTH = 4
SSD_CHUNK = 128
D_XBC = D_INNER + 2 * SSD_GROUPS * D_STATE
DIFF_HEADS = 8
DIFF_HEAD_DIM = 64
DIFF_V_DIM = 128
RMS_EPS = 1e-5
LN_EPS = 1e-5
LANES = 128
VMEM_LIMIT = 56 * 1024 * 1024

_NT = (((1,), (1,)), ((), ()))
_TN = (((0,), (0,)), ((), ()))


def _cparams(*sem):
    return pltpu.CompilerParams(dimension_semantics=sem, vmem_limit_bytes=VMEM_LIMIT)


def _mm_kernel(a_ref, w_ref, o_ref):
    o_ref[...] = jnp.dot(a_ref[...], w_ref[...], preferred_element_type=F32).astype(o_ref.dtype)


def _mm_scaled_kernel(a_ref, w_ref, c_ref, o_ref):
    o_ref[...] = (jnp.dot(a_ref[...], w_ref[...], preferred_element_type=F32) * c_ref[...]).astype(o_ref.dtype)


def _matmul(a, w, out_dtype, tm, tn, col_scale=None):
    m, k = a.shape
    n = w.shape[1]
    assert m % tm == 0 and n % tn == 0
    in_specs = [pl.BlockSpec((tm, k), lambda j, i: (i, 0)), pl.BlockSpec((k, tn), lambda j, i: (0, j))]
    operands = [a, w]
    if col_scale is not None:
        in_specs.append(pl.BlockSpec((1, tn), lambda j, i: (0, j)))
        operands.append(col_scale.reshape(1, n))
    return pl.pallas_call(
        _mm_kernel if col_scale is None else _mm_scaled_kernel,
        grid=(n // tn, m // tm),
        in_specs=in_specs,
        out_specs=pl.BlockSpec((tm, tn), lambda j, i: (i, j)),
        out_shape=jax.ShapeDtypeStruct((m, n), out_dtype),
        compiler_params=_cparams("arbitrary", "arbitrary"),
        name="matmul",
    )(*operands)


def _silu(x):
    return x * (1.0 / (1.0 + jnp.exp(-x)))


HALO = 16


def _mm_conv_kernel(tiles_per_seq, a_ref, halo_ref, w_ref, cw_ref, cb_ref, o_ref):
    tm = a_ref.shape[0]
    w = w_ref[...]
    h = jnp.dot(a_ref[...], w, preferred_element_type=F32)
    first = (pl.program_id(1) % tiles_per_seq) == 0
    h_prev = jnp.where(first, 0.0, jnp.dot(halo_ref[...], w, preferred_element_type=F32))
    hc = jnp.concatenate([h_prev, h], axis=0)
    acc = cb_ref[...] + cw_ref[CONV_WIDTH - 1:CONV_WIDTH, :] * h
    for kk in range(CONV_WIDTH - 1):
        off = HALO - (CONV_WIDTH - 1) + kk
        acc = acc + cw_ref[kk:kk + 1, :] * hc[off:off + tm, :]
    o_ref[...] = _silu(acc)


def _matmul_conv_silu(a, w, cw, cb, seq, tm=1024, tn=1024):
    t, k = a.shape
    c = w.shape[1]
    assert seq % tm == 0 and c % tn == 0 and tm % HALO == 0
    hb = tm // HALO
    return pl.pallas_call(
        functools.partial(_mm_conv_kernel, seq // tm),
        grid=(c // tn, t // tm),
        in_specs=[
            pl.BlockSpec((tm, k), lambda j, i: (i, 0)),
            pl.BlockSpec((HALO, k), lambda j, i: (jnp.maximum(i * hb - 1, 0), 0)),
            pl.BlockSpec((k, tn), lambda j, i: (0, j)),
            pl.BlockSpec((CONV_WIDTH, tn), lambda j, i: (0, j)),
            pl.BlockSpec((1, tn), lambda j, i: (0, j)),
        ],
        out_specs=pl.BlockSpec((tm, tn), lambda j, i: (i, j)),
        out_shape=jax.ShapeDtypeStruct((t, c), F32),
        compiler_params=_cparams("arbitrary", "arbitrary"),
        name="matmul_conv_silu",
    )(a, a, w, cw, cb.reshape(1, c))


def _split2(v):
    hi = v.astype(BF16)
    lo = (v - hi.astype(F32)).astype(BF16)
    return jnp.concatenate([hi, lo], axis=1)


def _ssd_kernel(xbc_ref, z_ref, dt_ref, dtb_ref, alog_ref, dskip_ref, e2_ref, nw_ref, o_ref, state_ref):
    L = SSD_CHUNK
    gw = D_INNER // SSD_GROUPS

    @pl.when(pl.program_id(1) == 0)
    def _():
        state_ref[...] = jnp.zeros_like(state_ref)

    x_dt = dt_ref[...] + dtb_ref[...]
    dt = jnp.maximum(x_dt, 0.0) + jnp.log1p(jnp.exp(-jnp.abs(x_dt)))
    a_neg = -jnp.exp(alog_ref[...])
    A = dt * a_neg
    row = lax.broadcasted_iota(jnp.int32, (L, L), 0)
    col = lax.broadcasted_iota(jnp.int32, (L, L), 1)
    tril = col <= row
    acs = jnp.dot(tril.astype(F32), A, precision=lax.Precision.HIGHEST, preferred_element_type=F32)
    a_last = acs[L - 1:L, :]
    w_state = dt * jnp.exp(a_last - acs)
    exp_a = jnp.exp(acs)
    cd = jnp.exp(jnp.broadcast_to(a_last, (8, LANES)))
    e2 = e2_ref[...]
    w_x = jnp.dot(_split2(w_state), e2, preferred_element_type=F32)
    expa_x = jnp.dot(_split2(exp_a), e2, preferred_element_type=F32)
    cd_x = jnp.dot(_split2(cd), e2, preferred_element_type=F32)[0:1, :]
    acs_t = acs.T
    dt_t = dt.T
    lane = lax.broadcasted_iota(jnp.int32, (L, LANES), 1)
    lo_half = lane < SSD_HEAD_DIM

    for g in range(SSD_GROUPS):
        bg = xbc_ref[:, D_INNER + g * D_STATE:D_INNER + (g + 1) * D_STATE].astype(BF16)
        cg = xbc_ref[:, D_INNER + SSD_GROUPS * D_STATE + g * D_STATE:D_INNER + SSD_GROUPS * D_STATE + (g + 1) * D_STATE].astype(BF16)
        xg = xbc_ref[:, g * gw:(g + 1) * gw]
        cb = lax.dot_general(cg, bg, _NT, preferred_element_type=F32)
        sg = state_ref[g]
        y_off = jnp.dot(cg, sg.astype(BF16), preferred_element_type=F32) * expa_x[:, g * gw:(g + 1) * gw]
        xw = (xg * w_x[:, g * gw:(g + 1) * gw]).astype(BF16)
        new_state = lax.dot_general(bg, xw, _TN, preferred_element_type=F32)
        state_ref[g] = sg * cd_x[:, g * gw:(g + 1) * gw] + new_state
        y_pairs = []
        for j in range(gw // LANES):
            ms = []
            for hh in range(2):
                h = g * (gw // SSD_HEAD_DIM) + 2 * j + hh
                seg = acs[:, h:h + 1] - acs_t[h:h + 1, :]
                ms.append(cb * jnp.exp(jnp.where(tril, seg, -jnp.inf)) * dt_t[h:h + 1, :])
            lhs = jnp.concatenate(ms, axis=1).astype(BF16)
            xp = xg[:, j * LANES:(j + 1) * LANES]
            rhs = jnp.concatenate([jnp.where(lo_half, xp, 0.0), jnp.where(lo_half, 0.0, xp)], axis=0).astype(BF16)
            y_pairs.append(jnp.dot(lhs, rhs, preferred_element_type=F32))
        y = jnp.concatenate(y_pairs, axis=1) + y_off + dskip_ref[:, g * gw:(g + 1) * gw] * xg
        y = y * _silu(z_ref[:, g * gw:(g + 1) * gw])
        ms_ = jnp.mean(y * y, axis=1, keepdims=True)
        y = y * lax.rsqrt(ms_ + RMS_EPS) * nw_ref[:, g * gw:(g + 1) * gw]
        o_ref[:, g * gw:(g + 1) * gw] = y.astype(o_ref.dtype)


def _head_expand_matrix():
    head_of_lane = jnp.arange(D_INNER) // SSD_HEAD_DIM
    e = (jnp.arange(LANES)[:, None] == head_of_lane[None, :]).astype(BF16)
    return jnp.concatenate([e, e], axis=0)


def _pad_lanes(v, fill=0.0):
    return jnp.pad(v.astype(F32), (0, LANES - v.shape[0]), constant_values=fill).reshape(1, LANES)


def _ssd(xbc_act, z_dt, dt_bias, a_log, d_skip, norm_w, batch, seq):
    t = xbc_act.shape[0]
    nchunk = seq // SSD_CHUNK
    row = lambda b, c: (b * nchunk + c, 0)
    dt_col = lambda b, c: (b * nchunk + c, D_INNER // LANES)
    const = lambda b, c: (0, 0)
    dskip_x = jnp.repeat(d_skip.astype(F32), SSD_HEAD_DIM).reshape(1, D_INNER)
    return pl.pallas_call(
        _ssd_kernel,
        grid=(batch, nchunk),
        in_specs=[
            pl.BlockSpec((SSD_CHUNK, D_XBC), row),
            pl.BlockSpec((SSD_CHUNK, D_INNER), row),
            pl.BlockSpec((SSD_CHUNK, LANES), dt_col),
            pl.BlockSpec((1, LANES), const),
            pl.BlockSpec((1, LANES), const),
            pl.BlockSpec((1, D_INNER), const),
            pl.BlockSpec((2 * LANES, D_INNER), const),
            pl.BlockSpec((1, D_INNER), const),
        ],
        out_specs=pl.BlockSpec((SSD_CHUNK, D_INNER), row),
        out_shape=jax.ShapeDtypeStruct((t, D_INNER), BF16),
        scratch_shapes=[pltpu.VMEM((SSD_GROUPS, D_STATE, D_INNER // SSD_GROUPS), F32)],
        compiler_params=_cparams("arbitrary", "arbitrary"),
        name="ssd",
    )(xbc_act, z_dt, z_dt, _pad_lanes(dt_bias), _pad_lanes(a_log), dskip_x, _head_expand_matrix(),
      norm_w.astype(F32).reshape(1, D_INNER))


def _attn_kernel(lambda_init, q_ref, k_ref, v_ref, lq_ref, lk_ref, sw_ref, o_ref, q2_ref, s_ref, smax_ref, m_ref, l_ref,
                 acc_ref):
    tq = q_ref.shape[0]
    qi = pl.program_id(2)
    lane = lax.broadcasted_iota(jnp.int32, (tq, LANES), 1)
    q = q_ref[...]
    zero = jnp.zeros_like(q)
    q2_ref[0:tq, :] = jnp.where(lane < DIFF_HEAD_DIM, q, zero)
    q2_ref[tq:2 * tq, :] = jnp.where(lane < DIFF_HEAD_DIM, zero, q)
    m_ref[...] = jnp.full_like(m_ref, -jnp.inf)
    l_ref[...] = jnp.zeros_like(l_ref)
    acc_ref[...] = jnp.zeros_like(acc_ref)

    def produce(kj, slot, keep_all):
        k = k_ref[pl.ds(pl.multiple_of(kj * tq, tq), tq), :]
        s_blk = lax.dot_general(k, q2_ref[...], _NT, preferred_element_type=F32)
        if keep_all is not True:
            kpos = lax.broadcasted_iota(jnp.int32, s_blk.shape, 0)
            qpos = lax.broadcasted_iota(jnp.int32, s_blk.shape, 1) & (tq - 1)
            s_blk = jnp.where((kpos <= qpos) | keep_all, s_blk, -jnp.inf)
        s_ref[slot] = s_blk
        smax_ref[slot] = jnp.max(s_blk, axis=0, keepdims=True)

    def consume(kj, slot):
        v = v_ref[pl.ds(pl.multiple_of(kj * tq, tq), tq), :]
        m_old = m_ref[...]
        m_new = jnp.maximum(m_old, smax_ref[slot])
        alpha = jnp.exp2(m_old - m_new)
        p = jnp.exp2(s_ref[slot] - m_new)
        l_ref[...] = alpha * l_ref[...] + jnp.sum(p, axis=0, keepdims=True)
        acc_ref[...] = acc_ref[...] * alpha + lax.dot_general(v, p.astype(BF16), _TN, preferred_element_type=F32)
        m_ref[...] = m_new

    def by_parity(idx, fn):
        lax.cond((idx & 1) == 0, lambda: fn(0), lambda: fn(1))

    produce(0, 0, qi > 0)

    def body(kj, carry):
        def step(cur):
            produce(kj + 1, 1 - cur, True)
            consume(kj, cur)
        by_parity(kj, step)
        return carry

    lax.fori_loop(0, qi - 1, body, 0)

    @pl.when(qi >= 1)
    def _():
        def step(cur):
            produce(qi, 1 - cur, False)
            consume(qi - 1, cur)
        by_parity(qi - 1, step)

    by_parity(qi, lambda cur: consume(qi, cur))

    o_all = acc_ref[...] / l_ref[...]
    lam = (jnp.exp(jnp.sum(lq_ref[0:1, :] * lk_ref[0:1, :], axis=1, keepdims=True))
           - jnp.exp(jnp.sum(lq_ref[1:2, :] * lk_ref[1:2, :], axis=1, keepdims=True)) + lambda_init)
    o = o_all[:, :tq] - lam * o_all[:, tq:]
    ms = jnp.mean(o * o, axis=0, keepdims=True)
    o = o * lax.rsqrt(ms + RMS_EPS) * sw_ref[...] * (1.0 - lambda_init)
    o_ref[...] = o.T.astype(o_ref.dtype)


def _diff_attention(qkv, lam_q, lam_k, subln_w, lambda_init, batch, seq, tq=512):
    t = qkv.shape[0]
    nq = seq // tq
    nh = DIFF_HEADS
    assert tq & (tq - 1) == 0 and seq % tq == 0
    return pl.pallas_call(
        functools.partial(_attn_kernel, lambda_init),
        grid=(batch, nh, nq),
        in_specs=[
            pl.BlockSpec((tq, LANES), lambda b, h, i: (b * nq + i, h)),
            pl.BlockSpec((seq, LANES), lambda b, h, i: (b, nh + h)),
            pl.BlockSpec((seq, LANES), lambda b, h, i: (b, 2 * nh + h)),
            pl.BlockSpec((2, DIFF_HEAD_DIM), lambda b, h, i: (0, 0)),
            pl.BlockSpec((2, DIFF_HEAD_DIM), lambda b, h, i: (0, 0)),
            pl.BlockSpec((DIFF_V_DIM, 1), lambda b, h, i: (0, 0)),
        ],
        out_specs=pl.BlockSpec((tq, LANES), lambda b, h, i: (b * nq + i, h)),
        out_shape=jax.ShapeDtypeStruct((t, nh * DIFF_V_DIM), BF16),
        scratch_shapes=[pltpu.VMEM((2 * tq, LANES), BF16), pltpu.VMEM((2, tq, 2 * tq), F32),
                        pltpu.VMEM((2, 1, 2 * tq), F32), pltpu.VMEM((1, 2 * tq), F32), pltpu.VMEM((1, 2 * tq), F32),
                        pltpu.VMEM((DIFF_V_DIM, 2 * tq), F32)],
        compiler_params=_cparams("arbitrary", "arbitrary", "arbitrary"),
        name="diff_attention",
    )(qkv, qkv, qkv, lam_q.astype(F32), lam_k.astype(F32), subln_w.astype(F32).reshape(DIFF_V_DIM, 1))


def _layer_norm(v, g, b):
    mu = jnp.mean(v, axis=-1, keepdims=True)
    d = v - mu
    var = jnp.mean(d * d, axis=-1, keepdims=True)
    return d * lax.rsqrt(var + LN_EPS) * g + b


def _sigmoid(v):
    return 1.0 / (1.0 + jnp.exp(-v))


def _mix_kernel(alpha, ys_ref, oa_ref, gates_ref, x_ref, wssd_ref, wdiff_ref, wo_ref, gb_ref, g_ref, b_ref, o_ref):
    y_ssd = jnp.dot(ys_ref[...], wssd_ref[...], preferred_element_type=F32)
    y_att = jnp.dot(oa_ref[...], wdiff_ref[...], preferred_element_type=F32)
    ga = _sigmoid(gates_ref[:, :D_MODEL] + gb_ref[0:1, :])
    gb = _sigmoid(gates_ref[:, D_MODEL:] + gb_ref[1:2, :])
    mix = jnp.dot((ga * y_ssd + gb * y_att).astype(BF16), wo_ref[...], preferred_element_type=F32)
    o_ref[...] = _layer_norm(alpha * x_ref[...] + mix, g_ref[...], b_ref[...])


def _resident(shape):
    return pl.BlockSpec(shape, lambda *_: (0,) * len(shape))


def _mix(ys, oa, gates, x, w_ssd, w_diff, w_o, gate_bias, ln_g, ln_b, alpha, tm=512):
    t = x.shape[0]
    rows = lambda w: pl.BlockSpec((tm, w), lambda i: (i, 0))
    return pl.pallas_call(
        functools.partial(_mix_kernel, alpha),
        grid=(t // tm,),
        in_specs=[rows(D_INNER), rows(D_MODEL), rows(2 * D_MODEL), rows(D_MODEL),
                  _resident((D_INNER, D_MODEL)), _resident((D_MODEL, D_MODEL)), _resident((D_MODEL, D_MODEL)),
                  _resident((2, D_MODEL)), _resident((1, D_MODEL)), _resident((1, D_MODEL))],
        out_specs=rows(D_MODEL),
        out_shape=jax.ShapeDtypeStruct((t, D_MODEL), F32),
        compiler_params=_cparams("arbitrary"),
        name="mix_ln1",
    )(ys, oa, gates, x, w_ssd, w_diff, w_o, gate_bias.astype(F32), ln_g.reshape(1, -1), ln_b.reshape(1, -1))


def _cross_kernel(alpha, nheads, x_ref, kv_ref, wq_ref, wo_ref, g_ref, b_ref, o_ref):
    x = x_ref[...]
    hd = D_MODEL // nheads
    q = jnp.dot(x.astype(BF16), wq_ref[...], preferred_element_type=F32).astype(BF16)
    outs = []
    for h in range(nheads):
        k = kv_ref[:, h * hd:(h + 1) * hd]
        v = kv_ref[:, D_MODEL + h * hd:D_MODEL + (h + 1) * hd]
        s = lax.dot_general(q[:, h * hd:(h + 1) * hd], k, _NT, preferred_element_type=F32) * (hd ** -0.5)
        e = jnp.exp(s - jnp.max(s, axis=-1, keepdims=True))
        p = e / jnp.sum(e, axis=-1, keepdims=True)
        outs.append(jnp.dot(p.astype(BF16), v, preferred_element_type=F32))
    o = jnp.concatenate(outs, axis=1).astype(BF16)
    y = jnp.dot(o, wo_ref[...], preferred_element_type=F32)
    o_ref[...] = _layer_norm(alpha * x + y, g_ref[...], b_ref[...])


def _cross_attention(x, kv, w_cq, w_co, ln_g, ln_b, alpha, nheads, batch, seq, tm=512):
    t = x.shape[0]
    mlen = kv.shape[0] // batch
    per = seq // tm
    return pl.pallas_call(
        functools.partial(_cross_kernel, alpha, nheads),
        grid=(t // tm,),
        in_specs=[pl.BlockSpec((tm, D_MODEL), lambda i: (i, 0)),
                  pl.BlockSpec((mlen, 2 * D_MODEL), lambda i: (i // per, 0)),
                  _resident((D_MODEL, D_MODEL)), _resident((D_MODEL, D_MODEL)),
                  _resident((1, D_MODEL)), _resident((1, D_MODEL))],
        out_specs=pl.BlockSpec((tm, D_MODEL), lambda i: (i, 0)),
        out_shape=jax.ShapeDtypeStruct((t, D_MODEL), F32),
        compiler_params=_cparams("arbitrary"),
        name="cross_ln2",
    )(x, kv, w_cq, w_co, ln_g.reshape(1, -1), ln_b.reshape(1, -1))


PEER_HEADS = 8
PEER_N_KEYS = 128
PEER_TOPK = 16
PEER_HALF = 128
WG_PITCH = 136
STAIR_FULL = 4
STAIR_LO = 4
STAIR_ROWS = STAIR_FULL * PEER_TOPK + (PEER_TOPK - STAIR_FULL) * STAIR_LO


def _topk_rows(s, nrows):
    rid = lax.broadcasted_iota(jnp.int32, s.shape, 0).astype(F32)
    vals, idxs = [], []
    for _ in range(PEER_TOPK):
        mx = jnp.max(s, axis=0, keepdims=True)
        idx = jnp.min(jnp.where(s == mx, rid, float(nrows)), axis=0, keepdims=True)
        s = jnp.where(rid == idx, -jnp.inf, s)
        vals.append(mx)
        idxs.append(idx)
    return jnp.concatenate(vals, axis=0), jnp.concatenate(idxs, axis=0)


def _pick_rows(sel, table):
    out = jnp.zeros_like(sel)
    for a in range(PEER_TOPK):
        out = jnp.where(sel == float(a), table[a:a + 1, :], out)
    return out


def _select_experts(v0, i0, v1, i1):
    cand = jnp.concatenate([v0[a:a + 1, :] + v1 for a in range(STAIR_FULL)]
                           + [v0[a:a + 1, :] + v1[0:STAIR_LO, :] for a in range(STAIR_FULL, PEER_TOPK)], axis=0)
    best, row = _topk_rows(cand, STAIR_ROWS)
    tail = row - float(STAIR_FULL * PEER_TOPK)
    a_head = jnp.floor(row * (1.0 / PEER_TOPK))
    a_tail = jnp.floor(tail * (1.0 / STAIR_LO))
    in_head = row < float(STAIR_FULL * PEER_TOPK)
    a_sel = jnp.where(in_head, a_head, a_tail + float(STAIR_FULL))
    b_sel = jnp.where(in_head, row - a_head * PEER_TOPK, tail - a_tail * STAIR_LO)
    e = jnp.exp(best - best[0:1, :])
    return _pick_rows(a_sel, i0), _pick_rows(b_sel, i1), e / jnp.sum(e, axis=0, keepdims=True)


def _peer_kernel(alpha, ntiles, xr_ref, xd_ref, wq_ref, sk_ref, u_ref, v_ref, lg_ref, lb_ref, o_ref,
                 st_ref, tv_ref, ti_ref, it_ref, jt_ref, gt_ref, ir_ref, jr_ref, gr_ref, wg_ref, xb_ref):
    i = pl.program_id(0)
    jc = pl.program_id(1)
    tm = xr_ref.shape[0]
    ec = u_ref.shape[0]
    ngrp = tm // LANES
    nhh = 2 * PEER_HEADS
    has_dense = i >= 1
    has_route = i < ntiles

    @pl.when((jc == 0) & has_dense)
    def _():
        for g in range(ngrp):
            ir_ref[g * LANES:(g + 1) * LANES, :] = it_ref[g].T.astype(jnp.int32)
            jr_ref[g * LANES:(g + 1) * LANES, :] = jt_ref[g].T.astype(jnp.int32)
            gr_ref[g * LANES:(g + 1) * LANES, :] = gt_ref[g].T
        xb_ref[...] = xd_ref[...].astype(BF16)
        o_ref[...] = jnp.zeros_like(o_ref)
        rid = lax.broadcasted_iota(jnp.int32, (PEER_N_KEYS, LANES), 0)

        def token(t, carry):
            irow = ir_ref[pl.ds(t, 1), :]
            jrow = jr_ref[pl.ds(t, 1), :]
            g = gr_ref[pl.ds(t, 1), :]
            g_hi = g.astype(BF16).astype(F32)
            g_lo = g - g_hi
            a1 = jnp.where(rid == irow, 1.0, 0.0).astype(BF16)
            hit = rid == jrow
            lhs = jnp.concatenate([a1, a1], axis=1)
            rhs = jnp.concatenate([jnp.where(hit, g_hi, 0.0).astype(BF16), jnp.where(hit, g_lo, 0.0).astype(BF16)], axis=1)
            w = lax.dot_general(lhs, rhs, _NT, preferred_element_type=F32)
            wg_ref[pl.ds(pl.multiple_of(t * WG_PITCH, 8), PEER_N_KEYS), :] = w
            return carry

        lax.fori_loop(0, tm, token, 0, unroll=32)

    @pl.when((jc == 0) & (i == 0))
    def _():
        tv_ref[...] = jnp.zeros_like(tv_ref)
        ti_ref[...] = jnp.zeros_like(ti_ref)

    @pl.when((jc == 0) & has_route)
    def _():
        q = jnp.dot(xr_ref[...].astype(BF16), wq_ref[...], preferred_element_type=F32).astype(BF16)
        for hh in range(nhh):
            st_ref[hh] = lax.dot_general(sk_ref[hh], q[:, hh * PEER_HALF:(hh + 1) * PEER_HALF], _NT,
                                         preferred_element_type=F32)

    def route_scores(item):
        h = item % PEER_HEADS
        tok0 = pl.multiple_of((item // PEER_HEADS) * LANES, LANES)
        for half in range(2):
            vals, idxs = _topk_rows(st_ref[2 * h + half, :, pl.ds(tok0, LANES)], PEER_N_KEYS)
            tv_ref[item & 1, half] = vals
            ti_ref[item & 1, half] = idxs

    def route_select(item):
        slot = item & 1
        ii, jj, gate = _select_experts(tv_ref[slot, 0], ti_ref[slot, 0], tv_ref[slot, 1], ti_ref[slot, 1])
        g = item // PEER_HEADS
        r0 = pl.multiple_of((item % PEER_HEADS) * PEER_TOPK, PEER_TOPK)
        it_ref[g, pl.ds(r0, PEER_TOPK), :] = ii
        jt_ref[g, pl.ds(r0, PEER_TOPK), :] = jj
        gt_ref[g, pl.ds(r0, PEER_TOPK), :] = gate

    def route_item():
        route_select(jnp.maximum(jc - 1, 0))
        route_scores(jc)

    def dense_step():
        hid = lax.dot_general(xb_ref[...], u_ref[...], _NT, preferred_element_type=F32)
        act = 0.5 * hid * (1.0 + lax.erf(hid * (2.0 ** -0.5)))
        nblk = ec // LANES
        parts = []
        for cc in range(nblk):
            wrow = wg_ref[pl.ds(jc * nblk + cc, tm, stride=WG_PITCH), :]
            parts.append((wrow * act[:, cc * LANES:(cc + 1) * LANES]).astype(BF16))
        o_ref[...] += jnp.dot(jnp.concatenate(parts, axis=1), v_ref[...], preferred_element_type=F32)

    @pl.when(has_dense & has_route)
    def _():
        route_item()
        dense_step()

    @pl.when(jnp.logical_not(has_dense))
    def _():
        route_item()

    @pl.when(jnp.logical_not(has_route))
    def _():
        dense_step()

    last = pl.num_programs(1) - 1

    @pl.when((jc == last) & has_route)
    def _():
        route_select(last)

    @pl.when((jc == last) & has_dense)
    def _():
        o_ref[...] = _layer_norm(alpha * xd_ref[...] + o_ref[...], lg_ref[...], lb_ref[...])


def _peer(x, w_pq, sub_keys, u, v, ln_g, ln_b, alpha, tm=256):
    t = x.shape[0]
    ne = u.shape[0]
    ntiles = t // tm
    ngrp = tm // LANES
    nsteps = PEER_HEADS * ngrp
    ec = ne // nsteps
    assert ne == PEER_N_KEYS * PEER_N_KEYS and ne % nsteps == 0 and ec % LANES == 0
    once = dict(pipeline_mode=pl.Buffered(1))
    tile_prev = lambda i, j: (jnp.maximum(i - 1, 0), 0)
    return pl.pallas_call(
        functools.partial(_peer_kernel, alpha, ntiles),
        grid=(ntiles + 1, nsteps),
        in_specs=[pl.BlockSpec((tm, D_MODEL), lambda i, j: (jnp.minimum(i, ntiles - 1), 0), **once),
                  pl.BlockSpec((tm, D_MODEL), tile_prev, **once),
                  pl.BlockSpec((D_MODEL, 2 * PEER_HEADS * PEER_HALF), lambda i, j: (0, 0), **once),
                  pl.BlockSpec((2 * PEER_HEADS, PEER_N_KEYS, PEER_HALF), lambda i, j: (0, 0, 0), **once),
                  pl.BlockSpec((ec, D_MODEL), lambda i, j: (j, 0)), pl.BlockSpec((ec, D_MODEL), lambda i, j: (j, 0)),
                  _resident((1, D_MODEL)), _resident((1, D_MODEL))],
        out_specs=pl.BlockSpec((tm, D_MODEL), tile_prev, **once),
        out_shape=jax.ShapeDtypeStruct((t, D_MODEL), F32),
        scratch_shapes=[pltpu.VMEM((2 * PEER_HEADS, PEER_N_KEYS, tm), F32),
                        pltpu.VMEM((2, 2, PEER_TOPK, LANES), F32), pltpu.VMEM((2, 2, PEER_TOPK, LANES), F32),
                        pltpu.VMEM((ngrp, LANES, LANES), F32), pltpu.VMEM((ngrp, LANES, LANES), F32),
                        pltpu.VMEM((ngrp, LANES, LANES), F32),
                        pltpu.VMEM((tm, LANES), jnp.int32), pltpu.VMEM((tm, LANES), jnp.int32),
                        pltpu.VMEM((tm, LANES), F32),
                        pltpu.VMEM((tm * WG_PITCH, LANES), F32), pltpu.VMEM((tm, D_MODEL), BF16)],
        compiler_params=_cparams("arbitrary", "arbitrary"),
        name="peer_ln3",
    )(x, x, w_pq, sub_keys, u, v, ln_g.reshape(1, -1), ln_b.reshape(1, -1))


def kernel(x, mem, w_in, conv_w, conv_b, dt_bias, a_log, d_skip, ssd_norm_w, w_ssd_br, lam_q, lam_k, subln_w, w_diff_br, gate_bias, w_o, ln1_g, ln1_b, w_cq, w_ck, w_cv, w_co, ln2_g, ln2_b, w_pq, sub_keys, peer_u, peer_v, ln3_g, ln3_b):
    batch, seq, d_model = x.shape
    depth = w_in.shape[0]
    t = batch * seq
    assert d_model == D_MODEL and w_in.shape[2] == D_INNER + D_XBC + SSD_HEADS + 5 * D_MODEL
    assert seq % 512 == 0 and sub_keys.shape[1:] == (PEER_HEADS, 2, PEER_N_KEYS, PEER_HALF)
    alpha = (2 * depth) ** 0.25
    mem_heads = 4
    cuts = [D_INNER, D_INNER + D_XBC, D_INNER + D_XBC + SSD_HEADS]
    xf = x.reshape(t, d_model)
    memf = mem.reshape(batch * mem.shape[1], d_model).astype(BF16)
    for l in range(depth):
        lambda_init = 0.8 - 0.6 * math.exp(-0.3 * l)
        w = w_in[l].astype(BF16)
        w_dt = jnp.pad(w[:, cuts[1]:cuts[2]], ((0, 0), (0, LANES - SSD_HEADS)))
        xb = xf.astype(BF16)
        z_dt = _matmul(xb, jnp.concatenate([w[:, :cuts[0]], w_dt], axis=1), F32, 1024, D_INNER + LANES)
        xbc_act = _matmul_conv_silu(xb, w[:, cuts[0]:cuts[1]], conv_w[l].astype(F32), conv_b[l].astype(F32), seq)
        q_scale = jnp.concatenate([jnp.full((D_MODEL,), DIFF_HEAD_DIM ** -0.5 * math.log2(math.e), F32),
                                   jnp.ones((2 * D_MODEL,), F32)])
        qkv = _matmul(xb, w[:, cuts[2]:cuts[2] + 3 * D_MODEL], BF16, 1024, 1024, col_scale=q_scale)
        gates = _matmul(xb, w[:, cuts[2] + 3 * D_MODEL:], F32, 1024, 1024)
        y_ssd = _ssd(xbc_act, z_dt, dt_bias[l], a_log[l], d_skip[l], ssd_norm_w[l], batch, seq)
        o_att = _diff_attention(qkv, lam_q[l], lam_k[l], subln_w[l], lambda_init, batch, seq)
        xf = _mix(y_ssd, o_att, gates, xf, w_ssd_br[l].astype(BF16), w_diff_br[l].astype(BF16), w_o[l].astype(BF16),
                  gate_bias[l], ln1_g[l], ln1_b[l], alpha)
        kv = _matmul(memf, jnp.concatenate([w_ck[l], w_cv[l]], axis=1).astype(BF16), BF16, memf.shape[0], 1024)
        xf = _cross_attention(xf, kv, w_cq[l].astype(BF16), w_co[l].astype(BF16), ln2_g[l], ln2_b[l], alpha,
                              mem_heads, batch, seq)
        xf = _peer(xf, w_pq[l].astype(BF16), sub_keys[l].reshape(2 * PEER_HEADS, PEER_N_KEYS, PEER_HALF).astype(BF16),
                   peer_u[l].astype(BF16), peer_v[l].astype(BF16), ln3_g[l], ln3_b[l], alpha)
    return xf.reshape(batch, seq, d_model)
```
